```python
import math
import jax, jax.numpy as jnp
from jax import lax
import numpy as np

D_MODEL = 4096
BATCH = 4
SEQ = 2048
DEPTH = 4
DEC_BATCH = 128
DEC_SEQ = 8
PAST_LEN = 16384
PAGE_SIZE = 128

RW_HEAD = 64
RW_WIDTH = D_MODEL
RW_HEADS = RW_WIDTH // RW_HEAD
RW_DECAY_LORA = max(32, int(round(1.8 * D_MODEL ** 0.5 / 32)) * 32)
RW_AAA_LORA = max(32, int(round(1.8 * D_MODEL ** 0.5 / 32)) * 32)
RW_MV_LORA = max(32, int(round(1.3 * D_MODEL ** 0.5 / 32)) * 32)
RW_GATE_LORA = max(32, int(round(0.6 * D_MODEL ** 0.8 / 32)) * 32)
RW_COLS = 3 * RW_WIDTH + RW_DECAY_LORA + RW_AAA_LORA + RW_GATE_LORA
LNX_EPS = 64e-5
MB_INNER = 2 * D_MODEL
MB_HEAD = 64
MB_HEADS = MB_INNER // MB_HEAD
MB_GROUPS = 8
MB_HPG = MB_HEADS // MB_GROUPS
MB_STATE = 128
MB_CONV = 4
MB_CONV_DIM = MB_INNER + 2 * MB_GROUPS * MB_STATE
MB_COLS = MB_INNER + MB_CONV_DIM + MB_HEADS
SSD_CHUNK = 128
GATE_COLS = 2 * D_MODEL
IN_COLS = RW_COLS + MB_COLS + GATE_COLS
D_FF = -(-8 * D_MODEL // (3 * 256)) * 256
N_MOD = 6
NORM_EPS = 1e-6
GATED_NORM_EPS = 1e-5

kernel_name = 'hybrid_rwkv7_mamba2_adaln_decode_step'


def _rms(x, g, eps=NORM_EPS):
    xf = x.astype(jnp.float32)
    y = xf * lax.rsqrt(jnp.mean(xf * xf, axis=-1, keepdims=True) + eps)
    return (y * g.astype(jnp.float32)).astype(x.dtype)


def _wkv7(r, w, k, v, a, b, s0):
    def step(s, inp):
        rt, wt, kt, vt, at, bt_ = inp
        sa = jnp.einsum('bhij,bhj->bhi', s, at)
        s = s * wt[:, :, None, :] + sa[..., None] * bt_[:, :, None, :] + vt[..., None] * kt[:, :, None, :]
        return s, jnp.einsum('bhij,bhj->bhi', s, rt)
    xs = tuple(t.swapaxes(0, 1) for t in (r, w, k, v, a, b))
    s, ys = lax.scan(step, s0, xs)
    return ys.swapaxes(0, 1), s


def _rwkv7_branch(p, shift_prev, s0, v_first, mu, w0, w2, a0, a2, g2, k_k, k_a, r_k, lnx_w, lnx_b, vres):
    bt, L, _ = p.shape
    f32 = jnp.float32
    prev = jnp.concatenate([shift_prev[:, None, :].astype(p.dtype), p[:, :-1]], axis=1)
    q = p + (prev - p) * mu
    cuts = [RW_WIDTH, 2 * RW_WIDTH, 3 * RW_WIDTH, 3 * RW_WIDTH + RW_DECAY_LORA,
            3 * RW_WIDTH + RW_DECAY_LORA + RW_AAA_LORA]
    r, k, v, wd, ad, gd = jnp.split(q, cuts, axis=-1)
    logw = -jax.nn.softplus(-(w0 + jnp.tanh(wd) @ w2)) - 0.5
    a = jax.nn.sigmoid(a0 + ad @ a2)
    g = jax.nn.sigmoid(gd) @ g2
    if vres is None:
        v_first = v
    else:
        v0, v1, v2 = vres
        v = v + (v_first - v) * jax.nn.sigmoid(v0 + (v @ v1) @ v2)

    def heads(t):
        return t.reshape(bt, L, RW_HEADS, RW_HEAD).astype(f32)

    kk = heads(k * k_k)
    kk = kk / jnp.maximum(jnp.sqrt(jnp.sum(kk * kk, axis=-1, keepdims=True)), 1e-12)
    k = k * (1 + (a - 1) * k_a)
    rh, kh, vh, ah = heads(r), heads(k), heads(v), heads(a)
    decay = jnp.exp(-jnp.exp(heads(logw)))
    y, s_new = _wkv7(rh, decay, kh, vh, -kk, kk * ah, s0.astype(f32))
    mean = jnp.mean(y, axis=-1, keepdims=True)
    var = jnp.mean(jnp.square(y - mean), axis=-1, keepdims=True)
    y = ((y - mean) * lax.rsqrt(var + LNX_EPS)).reshape(bt, L, RW_WIDTH)
    y = y * lnx_w.astype(f32) + lnx_b.astype(f32)
    bonus = jnp.sum(rh * kh * r_k.astype(f32), axis=-1, keepdims=True) * vh
    y = (y + bonus.reshape(bt, L, RW_WIDTH)).astype(p.dtype) * g
    return y, p[:, -1], s_new.astype(s0.dtype), v_first


def _ssd(x, dt, a, bm, cm, h0):
    bt, L = x.shape[0], x.shape[1]
    q = math.gcd(L, SSD_CHUNK)
    nc = L // q
    f32 = jnp.float32

    def chunks(t):
        return t.astype(f32).reshape((bt, nc, q) + t.shape[2:]).swapaxes(0, 1)

    mask = jnp.tril(jnp.ones((q, q), dtype=bool))[None, :, :, None, None]

    def step(h, inp):
        xq, dtq, bq, cq = inp
        acum = jnp.cumsum(dtq * a, axis=1)
        seg = jnp.where(mask, acum[:, :, None] - acum[:, None, :], -jnp.inf)
        wts = jnp.einsum('blgn,bsgn->blsg', cq, bq)[..., None] * jnp.exp(seg)
        xdt = xq * dtq[..., None]
        y = jnp.einsum('blsgr,bsgrp->blgrp', wts, xdt)
        y = y + jnp.einsum('blgn,bgrpn->blgrp', cq, h) * jnp.exp(acum)[..., None]
        tail = jnp.exp(acum[:, -1:] - acum)
        h = h * jnp.exp(acum[:, -1])[..., None, None] + jnp.einsum('bsgn,bsgrp->bgrpn', bq, xdt * tail[..., None])
        return h, y

    h, ys = lax.scan(step, h0.astype(f32), (chunks(x), chunks(dt), chunks(bm), chunks(cm)))
    return ys.swapaxes(0, 1).reshape(x.shape), h


def _mamba2_branch(p, conv_prev, s0, conv_w, conv_b, dt_bias, a_log, d_skip, norm_w):
    bt, L, _ = p.shape
    f32 = jnp.float32
    z, xbc, dt_raw = jnp.split(p, [MB_INNER, MB_INNER + MB_CONV_DIM], axis=-1)
    xpad = jnp.concatenate([conv_prev.astype(p.dtype), xbc], axis=1)
    conv = conv_b + sum(xpad[:, i:i + L] * conv_w[i] for i in range(MB_CONV))
    xbc = jax.nn.silu(conv)
    xs, bm, cm = jnp.split(xbc, [MB_INNER, MB_INNER + MB_GROUPS * MB_STATE], axis=-1)
    xs = xs.reshape(bt, L, MB_GROUPS, MB_HPG, MB_HEAD)
    bm = bm.reshape(bt, L, MB_GROUPS, MB_STATE)
    cm = cm.reshape(bt, L, MB_GROUPS, MB_STATE)
    dt = jax.nn.softplus((dt_raw + dt_bias).astype(f32)).reshape(bt, L, MB_GROUPS, MB_HPG)
    a = -jnp.exp(a_log.astype(f32)).reshape(MB_GROUPS, MB_HPG)
    h0 = s0.reshape(bt, MB_GROUPS, MB_HPG, MB_HEAD, MB_STATE)
    y, h = _ssd(xs, dt, a, bm, cm, h0)
    y = y + xs.astype(f32) * d_skip.astype(f32).reshape(MB_GROUPS, MB_HPG, 1)
    gs = (bt, L, MB_GROUPS, MB_INNER // MB_GROUPS)
    gated = y.reshape(gs) * jax.nn.silu(z.astype(f32)).reshape(gs)
    gated = gated * lax.rsqrt(jnp.mean(gated * gated, axis=-1, keepdims=True) + GATED_NORM_EPS)
    out = (gated.reshape(bt, L, MB_INNER) * norm_w.astype(f32)).astype(p.dtype)
    return out, xpad[:, L:], h.reshape(bt, MB_HEADS, MB_HEAD, MB_STATE).astype(s0.dtype)


def _trunk(x, c, st_shift, st_wkv, st_conv, st_ssm, w):
    bt = x.shape[0]
    v_first = None
    shifts, wkvs, convs, ssms = [], [], [], []
    for l in range(DEPTH):
        mod = (jax.nn.silu(c) @ w['ada_w'][l] + w['ada_b'][l]).reshape(bt, N_MOD, 1, D_MODEL)
        shift1, scale1, gate1, shift2, scale2, gate2 = (mod[:, i] for i in range(N_MOD))
        h = _rms(x, w['norm1'][l]) * (1 + scale1) + shift1
        proj = h @ w['w_in'][l]
        p_rw, p_mb, p_gate = jnp.split(proj, [RW_COLS, RW_COLS + MB_COLS], axis=-1)
        vres = None if l == 0 else (w['rw_v0'][l - 1], w['rw_v1'][l - 1], w['rw_v2'][l - 1])
        y_a, sh_new, wkv_new, v_first = _rwkv7_branch(
            p_rw, st_shift[l], st_wkv[l], v_first, w['rw_mu'][l], w['rw_w0'][l], w['rw_w2'][l],
            w['rw_a0'][l], w['rw_a2'][l], w['rw_g2'][l], w['rw_kk'][l], w['rw_ka'][l], w['rw_rk'][l],
            w['rw_lnx_w'][l], w['rw_lnx_b'][l], vres)
        y_b, conv_new, ssm_new = _mamba2_branch(
            p_mb, st_conv[l], st_ssm[l], w['mb_conv_w'][l], w['mb_conv_b'][l], w['mb_dt_bias'][l],
            w['mb_a_log'][l], w['mb_d'][l], w['mb_norm'][l])
        gate_a, gate_b = jnp.split(jax.nn.sigmoid(p_gate), 2, axis=-1)
        merged = gate_a * (y_a @ w['w_proj_a'][l]) + gate_b * (y_b @ w['w_proj_b'][l])
        x = x + gate1 * (merged @ w['w_out'][l])
        h2 = _rms(x, w['norm2'][l]) * (1 + scale2) + shift2
        ff = (jax.nn.silu(h2 @ w['ffn_gate'][l]) * (h2 @ w['ffn_up'][l])) @ w['ffn_down'][l]
        x = x + gate2 * ff
        shifts.append(sh_new)
        wkvs.append(wkv_new)
        convs.append(conv_new)
        ssms.append(ssm_new)
    return _rms(x, w['norm_f']), jnp.stack(shifts), jnp.stack(wkvs), jnp.stack(convs), jnp.stack(ssms)


def setup_inputs(seed: int = 0) -> dict:
    key = jax.random.key(seed)
    ks = iter(jax.random.split(key, 64))

    def nrm(shape, s):
        return jax.random.normal(next(ks), shape, jnp.float32) * s

    def uni(shape, lo, hi):
        return jax.random.uniform(next(ks), shape, jnp.float32, lo, hi)

    D = D_MODEL
    dvm = max(DEPTH - 1, 0)
    dt0 = jnp.exp(uni((DEPTH, MB_HEADS), math.log(1e-3), math.log(1e-1)))
    return {
        'x_prompt': nrm((BATCH, SEQ, D), 1.0),
        'x_sample': nrm((DEC_BATCH, DEC_SEQ, D), 1.0),
        'c_prompt': nrm((BATCH, D), 1.0),
        'c_sample': nrm((DEC_BATCH, D), 1.0),
        'state_rwkv_shift': nrm((DEPTH, DEC_BATCH, RW_COLS), 1.0),
        'state_rwkv_wkv': nrm((DEPTH, DEC_BATCH, RW_HEADS, RW_HEAD, RW_HEAD), 0.1),
        'state_mamba_conv': nrm((DEPTH, DEC_BATCH, MB_CONV - 1, MB_CONV_DIM), 1.0),
        'state_mamba_ssm': nrm((DEPTH, DEC_BATCH, MB_HEADS, MB_HEAD, MB_STATE), 0.1),
        'ada_w': nrm((DEPTH, D, N_MOD * D), D ** -0.5),
        'ada_b': nrm((DEPTH, N_MOD * D), 0.01),
        'norm1': 1.0 + nrm((DEPTH, D), 0.05),
        'norm2': 1.0 + nrm((DEPTH, D), 0.05),
        'w_in': nrm((DEPTH, D, IN_COLS), D ** -0.5),
        'rw_mu': uni((DEPTH, RW_COLS), 0.0, 1.0),
        'rw_w0': uni((DEPTH, RW_WIDTH), -3.0, 1.0),
        'rw_w2': nrm((DEPTH, RW_DECAY_LORA, RW_WIDTH), 0.5 * RW_DECAY_LORA ** -0.5),
        'rw_a0': nrm((DEPTH, RW_WIDTH), 0.1),
        'rw_a2': nrm((DEPTH, RW_AAA_LORA, RW_WIDTH), 0.5 * RW_AAA_LORA ** -0.5),
        'rw_g2': nrm((DEPTH, RW_GATE_LORA, RW_WIDTH), RW_GATE_LORA ** -0.5),
        'rw_kk': 0.85 + nrm((DEPTH, RW_WIDTH), 0.05),
        'rw_ka': 1.0 + nrm((DEPTH, RW_WIDTH), 0.05),
        'rw_rk': nrm((DEPTH, RW_HEADS, RW_HEAD), 0.1),
        'rw_lnx_w': 1.0 + nrm((DEPTH, RW_WIDTH), 0.05),
        'rw_lnx_b': nrm((DEPTH, RW_WIDTH), 0.01),
        'rw_v0': nrm((dvm, RW_WIDTH), 0.1),
        'rw_v1': nrm((dvm, RW_WIDTH, RW_MV_LORA), RW_WIDTH ** -0.5),
        'rw_v2': nrm((dvm, RW_MV_LORA, RW_WIDTH), 0.5 * RW_MV_LORA ** -0.5),
        'mb_conv_w': nrm((DEPTH, MB_CONV, MB_CONV_DIM), MB_CONV ** -0.5),
        'mb_conv_b': nrm((DEPTH, MB_CONV_DIM), 0.01),
        'mb_dt_bias': dt0 + jnp.log(-jnp.expm1(-dt0)),
        'mb_a_log': jnp.log(uni((DEPTH, MB_HEADS), 1.0, 16.0)),
        'mb_d': 1.0 + nrm((DEPTH, MB_HEADS), 0.05),
        'mb_norm': 1.0 + nrm((DEPTH, MB_INNER), 0.05),
        'w_proj_a': nrm((DEPTH, RW_WIDTH, D), RW_WIDTH ** -0.5),
        'w_proj_b': nrm((DEPTH, MB_INNER, D), MB_INNER ** -0.5),
        'w_out': nrm((DEPTH, D, D), D ** -0.5),
        'ffn_gate': nrm((DEPTH, D, D_FF), D ** -0.5),
        'ffn_up': nrm((DEPTH, D, D_FF), D ** -0.5),
        'ffn_down': nrm((DEPTH, D_FF, D), D_FF ** -0.5),
        'norm_f': 1.0 + nrm((D,), 0.05),
    }


def reference(x_prompt, x_sample, c_prompt, c_sample, state_rwkv_shift, state_rwkv_wkv,
              state_mamba_conv, state_mamba_ssm, ada_w, ada_b, norm1, norm2, w_in, rw_mu, rw_w0,
              rw_w2, rw_a0, rw_a2, rw_g2, rw_kk, rw_ka, rw_rk, rw_lnx_w, rw_lnx_b, rw_v0, rw_v1,
              rw_v2, mb_conv_w, mb_conv_b, mb_dt_bias, mb_a_log, mb_d, mb_norm, w_proj_a, w_proj_b,
              w_out, ffn_gate, ffn_up, ffn_down, norm_f):
    w = dict(ada_w=ada_w, ada_b=ada_b, norm1=norm1, norm2=norm2, w_in=w_in, rw_mu=rw_mu,
             rw_w0=rw_w0, rw_w2=rw_w2, rw_a0=rw_a0, rw_a2=rw_a2, rw_g2=rw_g2, rw_kk=rw_kk,
             rw_ka=rw_ka, rw_rk=rw_rk, rw_lnx_w=rw_lnx_w, rw_lnx_b=rw_lnx_b, rw_v0=rw_v0,
             rw_v1=rw_v1, rw_v2=rw_v2, mb_conv_w=mb_conv_w, mb_conv_b=mb_conv_b,
             mb_dt_bias=mb_dt_bias, mb_a_log=mb_a_log, mb_d=mb_d, mb_norm=mb_norm,
             w_proj_a=w_proj_a, w_proj_b=w_proj_b, w_out=w_out, ffn_gate=ffn_gate,
             ffn_up=ffn_up, ffn_down=ffn_down, norm_f=norm_f)
    bp = x_prompt.shape[0]
    z_shift = jnp.zeros((DEPTH, bp) + state_rwkv_shift.shape[2:], state_rwkv_shift.dtype)
    z_wkv = jnp.zeros((DEPTH, bp) + state_rwkv_wkv.shape[2:], state_rwkv_wkv.dtype)
    z_conv = jnp.zeros((DEPTH, bp) + state_mamba_conv.shape[2:], state_mamba_conv.dtype)
    z_ssm = jnp.zeros((DEPTH, bp) + state_mamba_ssm.shape[2:], state_mamba_ssm.dtype)
    y_prompt, sh_p, wkv_p, conv_p, ssm_p = _trunk(x_prompt, c_prompt, z_shift, z_wkv, z_conv, z_ssm, w)
    y_sample, sh_s, wkv_s, conv_s, ssm_s = _trunk(x_sample, c_sample, state_rwkv_shift, state_rwkv_wkv,
                                                  state_mamba_conv, state_mamba_ssm, w)
    return (y_prompt, y_sample, sh_p, wkv_p, conv_p, ssm_p, sh_s, wkv_s, conv_s, ssm_s)
```

```python
import functools

import jax
import jax.numpy as jnp
from jax import lax
from jax.experimental import pallas as pl
from jax.experimental.pallas import tpu as pltpu

f32 = jnp.float32
bf16 = jnp.bfloat16

D_MODEL = 4096
DEPTH = 4
RW_HEAD = 64
RW_HEADS = D_MODEL // RW_HEAD
RW_DECAY_LORA = 128
RW_AAA_LORA = 128
RW_MV_LORA = 96
RW_GATE_LORA = 480
RW_LORA = RW_DECAY_LORA + RW_AAA_LORA + RW_GATE_LORA
RW_COLS = 3 * D_MODEL + RW_LORA
LNX_EPS = 64e-5
MB_INNER = 2 * D_MODEL
MB_HEAD = 64
MB_HEADS = MB_INNER // MB_HEAD
MB_GROUPS = 8
MB_HPG = MB_HEADS // MB_GROUPS
MB_STATE = 128
MB_CONV = 4
MB_CONV_DIM = MB_INNER + 2 * MB_GROUPS * MB_STATE
MB_COLS = MB_INNER + MB_CONV_DIM + MB_HEADS
GATE_COLS = 2 * D_MODEL
D_FF = 11008
N_MOD = 6
NORM_EPS = 1e-6
GATED_NORM_EPS = 1e-5

LANES = 128
SUBLANES = 8
RW_LORA_PAD = 768
GATE_LORA_PAD = 512
MV_LORA_PAD = 128
RW_COLS_PAD = 3 * D_MODEL + RW_LORA_PAD
D_FF_PAD = 11264
VMEM_LIMIT = 48 * 1024 * 1024


def _cparams(sem):
    return pltpu.CompilerParams(dimension_semantics=sem, vmem_limit_bytes=VMEM_LIMIT)


def _mm_kernel(a_ref, b_ref, o_ref, acc_ref, *, act, nk):
    k = pl.program_id(2)

    @pl.when(k == 0)
    def _():
        acc_ref[...] = jnp.zeros_like(acc_ref)

    acc_ref[...] += jnp.dot(a_ref[...], b_ref[...], preferred_element_type=f32)

    @pl.when(k == nk - 1)
    def _():
        r = acc_ref[...]
        if act == "sigmoid":
            r = jax.nn.sigmoid(r)
        o_ref[...] = r.astype(o_ref.dtype)


def _mm(a, b, *, tm=1024, tn=1024, tk=1024, out_dtype=f32, act=None, name="mm"):
    m, kd = a.shape
    _, n = b.shape
    tm, tn, tk = min(tm, m), min(tn, n), min(tk, kd)
    assert m % tm == 0 and n % tn == 0 and kd % tk == 0, (a.shape, b.shape, tm, tn, tk)
    nk = kd // tk
    return pl.pallas_call(
        functools.partial(_mm_kernel, act=act, nk=nk),
        grid=(m // tm, n // tn, nk),
        in_specs=[pl.BlockSpec((tm, tk), lambda i, j, k: (i, k)),
                  pl.BlockSpec((tk, tn), lambda i, j, k: (k, j))],
        out_specs=pl.BlockSpec((tm, tn), lambda i, j, k: (i, j)),
        out_shape=jax.ShapeDtypeStruct((m, n), out_dtype),
        scratch_shapes=[pltpu.VMEM((tm, tn), f32)],
        compiler_params=_cparams(("parallel", "parallel", "arbitrary")),
        name=name,
    )(a, b)


def _swiglu_kernel(a_ref, bg_ref, bu_ref, o_ref, accg_ref, accu_ref, *, nk):
    k = pl.program_id(2)

    @pl.when(k == 0)
    def _():
        accg_ref[...] = jnp.zeros_like(accg_ref)
        accu_ref[...] = jnp.zeros_like(accu_ref)

    a = a_ref[...]
    accg_ref[...] += jnp.dot(a, bg_ref[...], preferred_element_type=f32)
    accu_ref[...] += jnp.dot(a, bu_ref[...], preferred_element_type=f32)

    @pl.when(k == nk - 1)
    def _():
        g = accg_ref[...]
        o_ref[...] = (g * jax.nn.sigmoid(g) * accu_ref[...]).astype(o_ref.dtype)


def _swiglu_mm(a, bg, bu, *, tm=1024, tn=1024, tk=1024):
    m, kd = a.shape
    _, n = bg.shape
    assert m % tm == 0 and n % tn == 0 and kd % tk == 0
    nk = kd // tk
    return pl.pallas_call(
        functools.partial(_swiglu_kernel, nk=nk),
        grid=(m // tm, n // tn, nk),
        in_specs=[pl.BlockSpec((tm, tk), lambda i, j, k: (i, k)),
                  pl.BlockSpec((tk, tn), lambda i, j, k: (k, j)),
                  pl.BlockSpec((tk, tn), lambda i, j, k: (k, j))],
        out_specs=pl.BlockSpec((tm, tn), lambda i, j, k: (i, j)),
        out_shape=jax.ShapeDtypeStruct((m, n), bf16),
        scratch_shapes=[pltpu.VMEM((tm, tn), f32), pltpu.VMEM((tm, tn), f32)],
        compiler_params=_cparams(("parallel", "parallel", "arbitrary")),
        name="swiglu_mm",
    )(a, bg, bu)


def _mod_kernel(c_ref, w_ref, b_ref, o_ref):
    c = c_ref[...]
    a = (c * jax.nn.sigmoid(c)).astype(bf16)
    o_ref[...] = jnp.dot(a, w_ref[...].astype(bf16), preferred_element_type=f32) + b_ref[...]


def _mod_all(c_all, ada_w, ada_b, *, tn=512):
    rows = c_all.shape[0]
    n = N_MOD * D_MODEL
    return pl.pallas_call(
        _mod_kernel,
        grid=(DEPTH, n // tn),
        in_specs=[pl.BlockSpec((rows, D_MODEL), lambda l, j: (0, 0)),
                  pl.BlockSpec((None, D_MODEL, tn), lambda l, j: (l, 0, j)),
                  pl.BlockSpec((None, 1, tn), lambda l, j: (l, 0, j))],
        out_specs=pl.BlockSpec((None, rows, tn), lambda l, j: (l, 0, j)),
        out_shape=jax.ShapeDtypeStruct((DEPTH, rows, n), f32),
        compiler_params=_cparams(("parallel", "parallel")),
        name="adaln_mod",
    )(c_all, ada_w, ada_b.reshape(DEPTH, 1, n))


def _resnorm_kernel(*refs, has_delta, final):
    it = iter(refs)
    x_ref = next(it)
    if has_delta:
        d_ref, gate_ref = next(it), next(it)
    g_ref = next(it)
    if not final:
        sc_ref, sh_ref = next(it), next(it)
    x = x_ref[...]
    if has_delta:
        x = x + gate_ref[...] * d_ref[...]
    y = x * lax.rsqrt(jnp.mean(x * x, axis=-1, keepdims=True) + NORM_EPS) * g_ref[...]
    if final:
        o_ref = next(it)
        o_ref[...] = y
    else:
        xo_ref, h_ref = next(it), next(it)
        xo_ref[...] = x
        h = y * (1.0 + sc_ref[...]) + sh_ref[...]
        h_ref[...] = h.reshape(h_ref.shape).astype(bf16)


def _resnorm(x3, delta3, modg, gate_at, g, l, sc_idx, sh_idx, *, final=False, tb=16):
    ng = x3.shape[0]
    assert ng % tb == 0
    has_delta = delta3 is not None
    row_spec = pl.BlockSpec((tb, SUBLANES, D_MODEL), lambda i: (i, 0, 0))

    def mod_spec(layer, idx):
        return pl.BlockSpec((None, tb, None, 1, D_MODEL), lambda i: (layer, i, idx, 0, 0))

    args, specs = [x3], [row_spec]
    if has_delta:
        args += [delta3, modg]
        specs += [row_spec, mod_spec(*gate_at)]
    args.append(g.reshape(1, 1, D_MODEL))
    specs.append(pl.BlockSpec((1, 1, D_MODEL), lambda i: (0, 0, 0)))
    if final:
        out_shape = jax.ShapeDtypeStruct(x3.shape, f32)
        out_specs = row_spec
    else:
        args += [modg, modg]
        specs += [mod_spec(l, sc_idx), mod_spec(l, sh_idx)]
        out_shape = [jax.ShapeDtypeStruct(x3.shape, f32), jax.ShapeDtypeStruct((ng * SUBLANES, D_MODEL), bf16)]
        out_specs = [row_spec, pl.BlockSpec((tb * SUBLANES, D_MODEL), lambda i: (i, 0))]
    return pl.pallas_call(
        functools.partial(_resnorm_kernel, has_delta=has_delta, final=final),
        grid=(ng // tb,), in_specs=specs, out_specs=out_specs, out_shape=out_shape,
        compiler_params=_cparams(("parallel",)),
        name="resnorm_final" if final else "resnorm",
    )(*args)


WKV_PAIRS = RW_HEAD // 2


def _wkv_kernel(r_ref, w_ref, k_ref, v_ref, a_ref, b_ref, s0_ref, y_ref, sout_ref, s_ref, *, tc, nc):
    c = pl.program_id(1)

    @pl.when(c == 0)
    def _():
        s_ref[...] = s0_ref[...]

    lane = lax.broadcasted_iota(jnp.int32, (WKV_PAIRS, LANES), 1)
    low = lane < RW_HEADS

    def both_halves(x):
        return x + pltpu.roll(x, RW_HEADS, axis=1)

    def step(t, carry):
        r_t, w_t, k_t, v_t, a_t, b_t = (ref[t] for ref in (r_ref, w_ref, k_ref, v_ref, a_ref, b_ref))
        wr = w_t * r_t
        sa = jnp.zeros((RW_HEAD, LANES), f32)
        yo = jnp.zeros((RW_HEAD, LANES), f32)
        for p in range(WKV_PAIRS):
            sp = s_ref[p]
            sa = sa + sp * a_t[p:p + 1, :]
            yo = yo + sp * wr[p:p + 1, :]
        sa = both_halves(sa)
        yo = both_halves(yo)
        br = both_halves(jnp.sum(b_t * r_t, axis=0, keepdims=True))
        kr = both_halves(jnp.sum(k_t * r_t, axis=0, keepdims=True))
        v_sw = pltpu.roll(v_t, RW_HEADS, axis=1)
        v_full = jnp.concatenate([jnp.where(low, v_t, v_sw), jnp.where(low, v_sw, v_t)], axis=0)
        for p in range(WKV_PAIRS):
            s_ref[p] = s_ref[p] * w_t[p:p + 1, :] + sa * b_t[p:p + 1, :] + v_full * k_t[p:p + 1, :]
        y = yo + sa * br + v_full * kr
        y_ref[t] = jnp.where(low, y[:WKV_PAIRS], y[WKV_PAIRS:])
        return carry

    lax.fori_loop(0, tc, step, 0)

    @pl.when(c == nc - 1)
    def _():
        sout_ref[...] = s_ref[...]


def _wkv(r, w, k, v, a, b, s0, *, tc):
    nseq, L = r.shape[0], r.shape[1]
    assert L % tc == 0
    nc = L // tc
    seq_spec = pl.BlockSpec((None, tc, WKV_PAIRS, LANES), lambda s, c: (s, c, 0, 0))
    st_spec = pl.BlockSpec((None, WKV_PAIRS, RW_HEAD, LANES), lambda s, c: (s, 0, 0, 0))
    return pl.pallas_call(
        functools.partial(_wkv_kernel, tc=tc, nc=nc),
        grid=(nseq, nc),
        in_specs=[seq_spec] * 6 + [st_spec],
        out_specs=[seq_spec, st_spec],
        out_shape=[jax.ShapeDtypeStruct(r.shape, f32), jax.ShapeDtypeStruct(s0.shape, f32)],
        scratch_shapes=[pltpu.VMEM((WKV_PAIRS, RW_HEAD, LANES), f32)],
        compiler_params=_cparams(("parallel", "arbitrary")),
        name="wkv7",
    )(r, w, k, v, a, b, s0)


def _wkv_state_to_kernel(s):
    n = s.shape[0]
    s = s.reshape(n, RW_HEADS, WKV_PAIRS, 2, WKV_PAIRS, 2)
    s = s.transpose(0, 4, 3, 2, 5, 1)
    return s.reshape(n, WKV_PAIRS, RW_HEAD, LANES)


def _wkv_state_from_kernel(s):
    n = s.shape[0]
    s = s.reshape(n, WKV_PAIRS, 2, WKV_PAIRS, 2, RW_HEADS)
    s = s.transpose(0, 5, 3, 2, 1, 4)
    return s.reshape(n, RW_HEADS, RW_HEAD, RW_HEAD)


def _ssd_kernel(x_ref, b_ref, c_ref, dt_ref, a_ref, h0_ref, y_ref, hout_ref, h_ref, *, q, nc):
    c = pl.program_id(2)

    @pl.when(c == 0)
    def _():
        h_ref[...] = h0_ref[...].reshape(h_ref.shape)

    nt = (((1,), (1,)), ((), ()))
    tn = (((0,), (0,)), ((), ()))
    hi = lax.Precision.HIGHEST
    row = lax.broadcasted_iota(jnp.int32, (q, q), 0)
    col = lax.broadcasted_iota(jnp.int32, (q, q), 1)
    causal = row >= col
    tri = causal.astype(f32)

    hg = h_ref[...]
    cg = c_ref[...].astype(bf16)
    bg = b_ref[...].astype(bf16)
    cb = lax.dot_general(cg, bg, nt, preferred_element_type=f32)
    y_state = lax.dot_general(cg, hg.astype(bf16), nt, preferred_element_type=f32)
    dt = dt_ref[...]
    dta = dt * a_ref[...]
    acum = jnp.dot(tri, dta, preferred_element_type=f32, precision=hi)
    acum_t = lax.dot_general(dta, tri, (((0,), (1,)), ((), ())), preferred_element_type=f32, precision=hi)
    x = x_ref[...]
    ys, xts = [], []
    for r in range(MB_HPG):
        a_col = acum[:, r:r + 1]
        seg = jnp.where(causal, a_col - acum_t[r:r + 1, :], -jnp.inf)
        wts = cb * jnp.exp(seg)
        xdt = x[:, r * MB_HEAD:(r + 1) * MB_HEAD] * dt[:, r:r + 1]
        y_r = jnp.dot(wts.astype(bf16), xdt.astype(bf16), preferred_element_type=f32)
        y_r = y_r + y_state[:, r * MB_HEAD:(r + 1) * MB_HEAD] * jnp.exp(a_col)
        ys.append(y_r)
        xts.append(xdt * jnp.exp(acum[q - 1:q, r:r + 1] - a_col))
    y_ref[...] = jnp.concatenate(ys, axis=1)
    xt = jnp.concatenate(xts, axis=1).astype(bf16)
    upd = lax.dot_general(xt, bg, tn, preferred_element_type=f32)
    for r in range(MB_HPG):
        sl = slice(r * MB_HEAD, (r + 1) * MB_HEAD)
        h_ref[sl, :] = hg[sl, :] * jnp.exp(acum[q - 1:q, r:r + 1]) + upd[sl, :]

    @pl.when(c == nc - 1)
    def _():
        hout_ref[...] = h_ref[...].reshape(hout_ref.shape)


def _ssd(x, bm, cm, dtg, ag, h0, *, nseq, L, q):
    nc = L // q
    gw = MB_HPG * MB_HEAD
    return pl.pallas_call(
        functools.partial(_ssd_kernel, q=q, nc=nc),
        grid=(nseq, MB_GROUPS, nc),
        in_specs=[pl.BlockSpec((q, gw), lambda s, g, c: (s * nc + c, g)),
                  pl.BlockSpec((q, MB_STATE), lambda s, g, c: (s * nc + c, g)),
                  pl.BlockSpec((q, MB_STATE), lambda s, g, c: (s * nc + c, g)),
                  pl.BlockSpec((None, q, MB_HPG), lambda s, g, c: (g, s * nc + c, 0)),
                  pl.BlockSpec((None, 1, MB_HPG), lambda s, g, c: (g, 0, 0)),
                  pl.BlockSpec((None, MB_HPG, MB_HEAD, MB_STATE), lambda s, g, c: (s, g, 0, 0))],
        out_specs=[pl.BlockSpec((q, gw), lambda s, g, c: (s * nc + c, g)),
                   pl.BlockSpec((None, MB_HPG, MB_HEAD, MB_STATE), lambda s, g, c: (s, g, 0, 0))],
        out_shape=[jax.ShapeDtypeStruct(x.shape, f32), jax.ShapeDtypeStruct(h0.shape, f32)],
        scratch_shapes=[pltpu.VMEM((gw, MB_STATE), f32)],
        compiler_params=_cparams(("parallel", "parallel", "arbitrary")),
        name="ssd",
    )(x, bm, cm, dtg, ag, h0)


def _perm_last(w):
    s = w.shape[:-1]
    return w.reshape(s + (RW_HEADS, RW_HEAD)).swapaxes(-1, -2).reshape(s + (D_MODEL,))


def _perm_rows(w):
    return w.reshape((RW_HEADS, RW_HEAD) + w.shape[1:]).swapaxes(0, 1).reshape(w.shape)


def _pad_last(w, n):
    return jnp.pad(w, [(0, 0)] * (w.ndim - 1) + [(0, n - w.shape[-1])])


def _perm_rw_cols(w):
    s = w.shape[:-1]
    rkv = _perm_last(w[..., :3 * D_MODEL].reshape(s + (3, D_MODEL))).reshape(s + (3 * D_MODEL,))
    return jnp.concatenate([rkv, _pad_last(w[..., 3 * D_MODEL:], RW_LORA_PAD)], axis=-1)


def _unperm_rw_cols(p):
    s = p.shape[:-1]
    rkv = _perm_last(p[..., :3 * D_MODEL].reshape(s + (3, D_MODEL))).reshape(s + (3 * D_MODEL,))
    return jnp.concatenate([rkv, p[..., 3 * D_MODEL:RW_COLS]], axis=-1)


def _seq_split(t, groups):
    out, off = [], 0
    for nseq, L in groups:
        out.append(t[off:off + nseq * L].reshape(nseq, L, t.shape[-1]))
        off += nseq * L
    return out


def _rwkv_branch(p_rw, shift_prev, wkv_prev, v_first, lw, groups):
    rows = p_rw.shape[0]
    parts = _seq_split(p_rw, groups)
    prev = jnp.concatenate(
        [jnp.concatenate([sp[:, None, :], pp[:, :-1]], axis=1).reshape(-1, RW_COLS_PAD)
         for sp, pp in zip(shift_prev, parts)], axis=0)
    shift_new = [pp[:, -1] for pp in parts]
    qv = p_rw + (prev - p_rw) * lw["mu"]
    r, k, v = (qv[:, i * D_MODEL:(i + 1) * D_MODEL] for i in range(3))
    o = 3 * D_MODEL
    wd = qv[:, o:o + RW_DECAY_LORA]
    ad = qv[:, o + RW_DECAY_LORA:o + RW_DECAY_LORA + RW_AAA_LORA]
    gd = qv[:, o + RW_DECAY_LORA + RW_AAA_LORA:o + RW_DECAY_LORA + RW_AAA_LORA + GATE_LORA_PAD]
    logw = -jax.nn.softplus(-(lw["w0"] + _mm(jnp.tanh(wd).astype(bf16), lw["w2"], name="lora_w"))) - 0.5
    a = jax.nn.sigmoid(lw["a0"] + _mm(ad.astype(bf16), lw["a2"], name="lora_a"))
    g = _mm(jax.nn.sigmoid(gd).astype(bf16), lw["g2"], name="lora_g")
    if lw["v1"] is None:
        v_first = v
    else:
        vlo = _mm(v.astype(bf16), lw["v1"], tk=D_MODEL, name="lora_v1")
        v = v + (v_first - v) * jax.nn.sigmoid(lw["v0"] + _mm(vlo.astype(bf16), lw["v2"], name="lora_v2"))

    def heads(t):
        return t.reshape(rows, RW_HEAD, RW_HEADS)

    kk = heads(k * lw["kk"])
    kk = kk / jnp.maximum(jnp.sqrt(jnp.sum(kk * kk, axis=1, keepdims=True)), 1e-12)
    kk = kk.reshape(rows, D_MODEL)
    k = k * (1 + (a - 1) * lw["ka"])
    decay = jnp.exp(-jnp.exp(logw))
    ins = [r, decay, k, v, -kk, kk * a]
    ys, s_new, off = [], [], 0
    for (nseq, L), s0 in zip(groups, wkv_prev):
        sl = [t[off:off + nseq * L].reshape(nseq, L, WKV_PAIRS, LANES) for t in ins]
        y_g, s_g = _wkv(*sl, s0, tc=min(L, 64))
        ys.append(y_g.reshape(nseq * L, D_MODEL))
        s_new.append(s_g)
        off += nseq * L
    y = heads(jnp.concatenate(ys, axis=0))
    mean = jnp.mean(y, axis=1, keepdims=True)
    var = jnp.mean(jnp.square(y - mean), axis=1, keepdims=True)
    y = ((y - mean) * lax.rsqrt(var + LNX_EPS)).reshape(rows, D_MODEL)
    y = y * lw["lnx_w"] + lw["lnx_b"]
    bonus = jnp.sum(heads(r * k * lw["rk"]), axis=1, keepdims=True) * heads(v)
    y = (y + bonus.reshape(rows, D_MODEL)) * g
    return y.astype(bf16), shift_new, s_new, v_first


def _mamba_branch(p_mb, conv_prev, ssm_prev, lw, groups):
    rows = p_mb.shape[0]
    z = p_mb[:, :MB_INNER]
    xbc = p_mb[:, MB_INNER:MB_INNER + MB_CONV_DIM]
    dt_raw = p_mb[:, MB_INNER + MB_CONV_DIM:]
    convs, conv_new = [], []
    for (nseq, L), cp, xp in zip(groups, conv_prev, _seq_split(xbc, groups)):
        xpad = jnp.concatenate([cp, xp], axis=1)
        conv = lw["conv_b"] + sum(xpad[:, i:i + L] * lw["conv_w"][i] for i in range(MB_CONV))
        convs.append(conv.reshape(nseq * L, MB_CONV_DIM))
        conv_new.append(xpad[:, L:])
    xbc = jax.nn.silu(jnp.concatenate(convs, axis=0))
    xs = xbc[:, :MB_INNER]
    bm = xbc[:, MB_INNER:MB_INNER + MB_GROUPS * MB_STATE]
    cm = xbc[:, MB_INNER + MB_GROUPS * MB_STATE:]
    dt = jax.nn.softplus(dt_raw + lw["dt_bias"])
    dtg = dt.reshape(rows, MB_GROUPS, MB_HPG).transpose(1, 0, 2)
    ag = (-jnp.exp(lw["a_log"])).reshape(MB_GROUPS, 1, MB_HPG)
    ys, ssm_new, off = [], [], 0
    for (nseq, L), h0 in zip(groups, ssm_prev):
        sl = slice(off, off + nseq * L)
        y_g, h_g = _ssd(xs[sl], bm[sl], cm[sl], dtg[:, sl], ag, h0, nseq=nseq, L=L, q=min(L, 128))
        ys.append(y_g)
        ssm_new.append(h_g)
        off += nseq * L
    y = jnp.concatenate(ys, axis=0)
    y = y + xs * jnp.repeat(lw["d"], MB_HEAD)
    gated = (y * jax.nn.silu(z)).reshape(rows, MB_GROUPS, MB_INNER // MB_GROUPS)
    gated = gated * lax.rsqrt(jnp.mean(gated * gated, axis=-1, keepdims=True) + GATED_NORM_EPS)
    out = gated.reshape(rows, MB_INNER) * lw["norm"]
    return out.astype(bf16), conv_new, ssm_new


def kernel(x_prompt, x_sample, c_prompt, c_sample, state_rwkv_shift, state_rwkv_wkv, state_mamba_conv, state_mamba_ssm, ada_w, ada_b, norm1, norm2, w_in, rw_mu, rw_w0, rw_w2, rw_a0, rw_a2, rw_g2, rw_kk, rw_ka, rw_rk, rw_lnx_w, rw_lnx_b, rw_v0, rw_v1, rw_v2, mb_conv_w, mb_conv_b, mb_dt_bias, mb_a_log, mb_d, mb_norm, w_proj_a, w_proj_b, w_out, ffn_gate, ffn_up, ffn_down, norm_f):
    bp, lp, _ = x_prompt.shape
    bs, ls, _ = x_sample.shape
    groups = ((bp, lp), (bs, ls))
    rows = bp * lp + bs * ls
    assert lp % SUBLANES == 0 and ls == SUBLANES

    nc_rows = bp + bs
    c_pad = -(-nc_rows // 16) * 16
    c_all = jnp.pad(jnp.concatenate([c_prompt, c_sample], axis=0), ((0, c_pad - nc_rows), (0, 0)))
    mod = _mod_all(c_all, ada_w, ada_b)[:, :nc_rows].reshape(DEPTH, nc_rows, N_MOD, 1, D_MODEL)
    modg = jnp.concatenate([jnp.repeat(mod[:, :bp], lp // SUBLANES, axis=1), mod[:, bp:]], axis=1)

    x3 = jnp.concatenate([x_prompt.reshape(-1, D_MODEL), x_sample.reshape(-1, D_MODEL)], axis=0)
    x3 = x3.reshape(rows // SUBLANES, SUBLANES, D_MODEL)

    zeros = functools.partial(jnp.zeros, dtype=f32)
    delta3, v_first = None, None
    shifts, wkvs, convs, ssms = [], [], [], []
    for l in range(DEPTH):
        w_l = w_in[l]
        w_rw = _perm_rw_cols(w_l[:, :RW_COLS]).astype(bf16)
        w_mb = w_l[:, RW_COLS:RW_COLS + MB_COLS].astype(bf16)
        w_gt = w_l[:, RW_COLS + MB_COLS:].astype(bf16)
        lw_rw = dict(
            mu=_perm_rw_cols(rw_mu[l]), w0=_perm_last(rw_w0[l]), a0=_perm_last(rw_a0[l]),
            w2=_perm_last(rw_w2[l]).astype(bf16), a2=_perm_last(rw_a2[l]).astype(bf16),
            g2=jnp.pad(_perm_last(rw_g2[l]), ((0, GATE_LORA_PAD - RW_GATE_LORA), (0, 0))).astype(bf16),
            kk=_perm_last(rw_kk[l]), ka=_perm_last(rw_ka[l]), rk=rw_rk[l].T.reshape(D_MODEL),
            lnx_w=_perm_last(rw_lnx_w[l]), lnx_b=_perm_last(rw_lnx_b[l]), v0=None, v1=None, v2=None)
        if l > 0:
            lw_rw.update(
                v0=_perm_last(rw_v0[l - 1]),
                v1=_pad_last(_perm_rows(rw_v1[l - 1]), MV_LORA_PAD).astype(bf16),
                v2=jnp.pad(_perm_last(rw_v2[l - 1]), ((0, MV_LORA_PAD - RW_MV_LORA), (0, 0))).astype(bf16))
        lw_mb = dict(conv_w=mb_conv_w[l], conv_b=mb_conv_b[l], dt_bias=mb_dt_bias[l], a_log=mb_a_log[l],
                     d=mb_d[l], norm=mb_norm[l])

        x3, h = _resnorm(x3, delta3, modg, (l - 1, 5), norm1[l], l, 1, 0)
        p_rw = _mm(h, w_rw, tn=768, name="in_rw")
        p_mb = _mm(h, w_mb, tn=640, name="in_mb")
        gates = _mm(h, w_gt, act="sigmoid", name="in_gate")

        shift_prev = [zeros((bp, RW_COLS_PAD)), _perm_rw_cols(state_rwkv_shift[l])]
        wkv_prev = [zeros((bp, WKV_PAIRS, RW_HEAD, LANES)), _wkv_state_to_kernel(state_rwkv_wkv[l])]
        y_a, sh_new, wkv_new, v_first = _rwkv_branch(p_rw, shift_prev, wkv_prev, v_first, lw_rw, groups)
        conv_prev = [zeros((bp, MB_CONV - 1, MB_CONV_DIM)), state_mamba_conv[l]]
        ssm_prev = [zeros((bp, MB_HEADS, MB_HEAD, MB_STATE)), state_mamba_ssm[l]]
        y_b, conv_new, ssm_new = _mamba_branch(p_mb, conv_prev, ssm_prev, lw_mb, groups)

        pa = _mm(y_a, _perm_rows(w_proj_a[l]).astype(bf16), name="proj_a")
        pb = _mm(y_b, w_proj_b[l].astype(bf16), name="proj_b")
        merged = (gates[:, :D_MODEL] * pa + gates[:, D_MODEL:] * pb).astype(bf16)
        mo = _mm(merged, w_out[l].astype(bf16), name="w_out")
        x3, h2 = _resnorm(x3, mo.reshape(x3.shape), modg, (l, 2), norm2[l], l, 4, 3)
        hid = _swiglu_mm(h2, _pad_last(ffn_gate[l], D_FF_PAD).astype(bf16), _pad_last(ffn_up[l], D_FF_PAD).astype(bf16))
        ff = _mm(hid, jnp.pad(ffn_down[l], ((0, D_FF_PAD - D_FF), (0, 0))).astype(bf16), name="ffn_down")
        delta3 = ff.reshape(x3.shape)

        shifts.append([_unperm_rw_cols(s) for s in sh_new])
        wkvs.append([_wkv_state_from_kernel(s) for s in wkv_new])
        convs.append(conv_new)
        ssms.append(ssm_new)

    y = _resnorm(x3, delta3, modg, (DEPTH - 1, 5), norm_f, None, None, None, final=True).reshape(rows, D_MODEL)
    y_prompt = y[:bp * lp].reshape(bp, lp, D_MODEL)
    y_sample = y[bp * lp:].reshape(bs, ls, D_MODEL)

    def stack(lst, gi):
        return jnp.stack([t[gi] for t in lst])

    return (y_prompt, y_sample,
            stack(shifts, 0), stack(wkvs, 0), stack(convs, 0), stack(ssms, 0),
            stack(shifts, 1), stack(wkvs, 1), stack(convs, 1), stack(ssms, 1))
```

```python
import functools

import jax
import jax.numpy as jnp
from jax import lax
from jax.experimental import pallas as pl
from jax.experimental.pallas import tpu as pltpu

f32 = jnp.float32
bf16 = jnp.bfloat16

D_MODEL = 4096
DEPTH = 4
RW_HEAD = 64
RW_HEADS = D_MODEL // RW_HEAD
RW_DECAY_LORA = 128
RW_AAA_LORA = 128
RW_MV_LORA = 96
RW_GATE_LORA = 480
RW_LORA = RW_DECAY_LORA + RW_AAA_LORA + RW_GATE_LORA
RW_COLS = 3 * D_MODEL + RW_LORA
LNX_EPS = 64e-5
MB_INNER = 2 * D_MODEL
MB_HEAD = 64
MB_HEADS = MB_INNER // MB_HEAD
MB_GROUPS = 8
MB_HPG = MB_HEADS // MB_GROUPS
MB_STATE = 128
MB_CONV = 4
MB_BC = MB_GROUPS * MB_STATE
MB_CONV_DIM = MB_INNER + 2 * MB_BC
MB_COLS = MB_INNER + MB_CONV_DIM + MB_HEADS
GATE_COLS = 2 * D_MODEL
D_FF = 11008
N_MOD = 6
NORM_EPS = 1e-6
GATED_NORM_EPS = 1e-5

LANES = 128
SUBLANES = 8
GATE_LORA_PAD = 512
MV_LORA_PAD = 128
D_FF_PAD = 11264
VMEM_LIMIT = 52 * 1024 * 1024

C_R, C_K, C_V = 0, D_MODEL, 2 * D_MODEL
C_GATE = 3 * D_MODEL
C_Z = C_GATE + GATE_COLS
C_X = C_Z + MB_INNER
C_B = C_X + MB_INNER
C_C = C_B + MB_BC
N_BIG = C_C + MB_BC
S_LORA_PAD = 768
S_DT = S_LORA_PAD
N_SMALL = 1024


def _cparams(sem):
    return pltpu.CompilerParams(dimension_semantics=sem, vmem_limit_bytes=VMEM_LIMIT)


def _mm_kernel(a_ref, b_ref, o_ref, acc_ref, *, nk, sig_lo, sig_hi):
    j, k = pl.program_id(1), pl.program_id(2)

    @pl.when(k == 0)
    def _():
        acc_ref[...] = jnp.zeros_like(acc_ref)

    acc_ref[...] += jnp.dot(a_ref[...], b_ref[...], preferred_element_type=f32)

    if sig_hi > sig_lo:
        @pl.when((k == nk - 1) & (j >= sig_lo) & (j < sig_hi))
        def _():
            o_ref[...] = jax.nn.sigmoid(acc_ref[...]).astype(o_ref.dtype)

        @pl.when((k == nk - 1) & ((j < sig_lo) | (j >= sig_hi)))
        def _():
            o_ref[...] = acc_ref[...].astype(o_ref.dtype)
    else:
        @pl.when(k == nk - 1)
        def _():
            o_ref[...] = acc_ref[...].astype(o_ref.dtype)


def _mm(a, b, *, tm=1024, tn=1024, tk=2048, out_dtype=f32, sigmoid_tiles=(0, 0), name="mm"):
    m, kd = a.shape
    _, n = b.shape
    tm, tn, tk = min(tm, m), min(tn, n), min(tk, kd)
    assert m % tm == 0 and n % tn == 0 and kd % tk == 0, (a.shape, b.shape, tm, tn, tk)
    nk = kd // tk
    return pl.pallas_call(
        functools.partial(_mm_kernel, nk=nk, sig_lo=sigmoid_tiles[0], sig_hi=sigmoid_tiles[1]),
        grid=(m // tm, n // tn, nk),
        in_specs=[pl.BlockSpec((tm, tk), lambda i, j, k: (i, k)),
                  pl.BlockSpec((tk, tn), lambda i, j, k: (k, j))],
        out_specs=pl.BlockSpec((tm, tn), lambda i, j, k: (i, j)),
        out_shape=jax.ShapeDtypeStruct((m, n), out_dtype),
        scratch_shapes=[pltpu.VMEM((tm, tn), f32)],
        compiler_params=_cparams(("parallel", "parallel", "arbitrary")),
        name=name,
    )(a, b)


def _swiglu_kernel(a_ref, bg_ref, bu_ref, o_ref, accg_ref, accu_ref, *, nk):
    k = pl.program_id(2)

    @pl.when(k == 0)
    def _():
        accg_ref[...] = jnp.zeros_like(accg_ref)
        accu_ref[...] = jnp.zeros_like(accu_ref)

    a = a_ref[...]
    accg_ref[...] += jnp.dot(a, bg_ref[...], preferred_element_type=f32)
    accu_ref[...] += jnp.dot(a, bu_ref[...], preferred_element_type=f32)

    @pl.when(k == nk - 1)
    def _():
        g = accg_ref[...]
        o_ref[...] = (g * jax.nn.sigmoid(g) * accu_ref[...]).astype(o_ref.dtype)


def _swiglu_mm(a, bg, bu, *, tm=1024, tn=1024, tk=2048):
    m, kd = a.shape
    _, n = bg.shape
    tm = min(tm, m)
    assert m % tm == 0 and n % tn == 0 and kd % tk == 0
    nk = kd // tk
    return pl.pallas_call(
        functools.partial(_swiglu_kernel, nk=nk),
        grid=(m // tm, n // tn, nk),
        in_specs=[pl.BlockSpec((tm, tk), lambda i, j, k: (i, k)),
                  pl.BlockSpec((tk, tn), lambda i, j, k: (k, j)),
                  pl.BlockSpec((tk, tn), lambda i, j, k: (k, j))],
        out_specs=pl.BlockSpec((tm, tn), lambda i, j, k: (i, j)),
        out_shape=jax.ShapeDtypeStruct((m, n), bf16),
        scratch_shapes=[pltpu.VMEM((tm, tn), f32), pltpu.VMEM((tm, tn), f32)],
        compiler_params=_cparams(("parallel", "parallel", "arbitrary")),
        name="swiglu_mm",
    )(a, bg, bu)


def _merge_kernel(ya_ref, wa_ref, yb_ref, wb_ref, ga_ref, gb_ref, o_ref, acca_ref, accb_ref, *, nka, nkb):
    k = pl.program_id(2)

    @pl.when(k == 0)
    def _():
        acca_ref[...] = jnp.zeros_like(acca_ref)
        accb_ref[...] = jnp.zeros_like(accb_ref)

    @pl.when(k < nka)
    def _():
        acca_ref[...] += jnp.dot(ya_ref[...], wa_ref[...], preferred_element_type=f32)

    @pl.when(k >= nka)
    def _():
        accb_ref[...] += jnp.dot(yb_ref[...], wb_ref[...], preferred_element_type=f32)

    @pl.when(k == nka + nkb - 1)
    def _():
        o_ref[...] = (ga_ref[...] * acca_ref[...] + gb_ref[...] * accb_ref[...]).astype(o_ref.dtype)


def _merge_mm(ya, wa, yb, wb, p_big, *, tm=1024, tn=1024, tk=1024):
    m, ka = ya.shape
    kb = yb.shape[1]
    n = wa.shape[1]
    tm = min(tm, m)
    assert m % tm == 0 and n % tn == 0 and ka % tk == 0 and kb % tk == 0 and C_GATE % tn == 0
    nka, nkb = ka // tk, kb // tk
    ga0, gb0 = C_GATE // tn, (C_GATE + n) // tn
    return pl.pallas_call(
        functools.partial(_merge_kernel, nka=nka, nkb=nkb),
        grid=(m // tm, n // tn, nka + nkb),
        in_specs=[pl.BlockSpec((tm, tk), lambda i, j, k: (i, jnp.minimum(k, nka - 1))),
                  pl.BlockSpec((tk, tn), lambda i, j, k: (jnp.minimum(k, nka - 1), j)),
                  pl.BlockSpec((tm, tk), lambda i, j, k: (i, jnp.maximum(k - nka, 0))),
                  pl.BlockSpec((tk, tn), lambda i, j, k: (jnp.maximum(k - nka, 0), j)),
                  pl.BlockSpec((tm, tn), lambda i, j, k: (i, ga0 + j)),
                  pl.BlockSpec((tm, tn), lambda i, j, k: (i, gb0 + j))],
        out_specs=pl.BlockSpec((tm, tn), lambda i, j, k: (i, j)),
        out_shape=jax.ShapeDtypeStruct((m, n), bf16),
        scratch_shapes=[pltpu.VMEM((tm, tn), f32), pltpu.VMEM((tm, tn), f32)],
        compiler_params=_cparams(("parallel", "parallel", "arbitrary")),
        name="merge_mm",
    )(ya, wa, yb, wb, p_big, p_big)


def _mod_kernel(c_ref, w_ref, b_ref, o_ref):
    c = c_ref[...]
    a = (c * jax.nn.sigmoid(c)).astype(bf16)
    o_ref[...] = jnp.dot(a, w_ref[...].astype(bf16), preferred_element_type=f32) + b_ref[...]


def _mod_all(c_all, ada_w, ada_b, *, tn=512):
    rows = c_all.shape[0]
    n = N_MOD * D_MODEL
    return pl.pallas_call(
        _mod_kernel,
        grid=(DEPTH, n // tn),
        in_specs=[pl.BlockSpec((rows, D_MODEL), lambda l, j: (0, 0)),
                  pl.BlockSpec((None, D_MODEL, tn), lambda l, j: (l, 0, j)),
                  pl.BlockSpec((None, 1, tn), lambda l, j: (l, 0, j))],
        out_specs=pl.BlockSpec((None, rows, tn), lambda l, j: (l, 0, j)),
        out_shape=jax.ShapeDtypeStruct((DEPTH, rows, n), f32),
        compiler_params=_cparams(("parallel", "parallel")),
        name="adaln_mod",
    )(c_all, ada_w, ada_b.reshape(DEPTH, 1, n))


def _resnorm_kernel(*refs, has_delta, final):
    it = iter(refs)
    x_ref = next(it)
    if has_delta:
        d_ref, gate_ref = next(it), next(it)
    g_ref = next(it)
    if not final:
        sc_ref, sh_ref = next(it), next(it)
    x = x_ref[...]
    if has_delta:
        x = x + gate_ref[...] * d_ref[...]
    y = x * lax.rsqrt(jnp.mean(x * x, axis=-1, keepdims=True) + NORM_EPS) * g_ref[...]
    if final:
        o_ref = next(it)
        o_ref[...] = y
    else:
        xo_ref, h_ref = next(it), next(it)
        xo_ref[...] = x
        h = y * (1.0 + sc_ref[...]) + sh_ref[...]
        h_ref[...] = h.reshape(h_ref.shape).astype(bf16)


def _resnorm(x3, delta3, modg, gate_at, g, l, sc_idx, sh_idx, *, final=False, tb=16):
    ng = x3.shape[0]
    assert ng % tb == 0
    has_delta = delta3 is not None
    row_spec = pl.BlockSpec((tb, SUBLANES, D_MODEL), lambda i: (i, 0, 0))

    def mod_spec(layer, idx):
        return pl.BlockSpec((None, tb, None, 1, D_MODEL), lambda i: (layer, i, idx, 0, 0))

    args, specs = [x3], [row_spec]
    if has_delta:
        args += [delta3, modg]
        specs += [row_spec, mod_spec(*gate_at)]
    args.append(g.reshape(1, 1, D_MODEL))
    specs.append(pl.BlockSpec((1, 1, D_MODEL), lambda i: (0, 0, 0)))
    if final:
        out_shape = jax.ShapeDtypeStruct(x3.shape, f32)
        out_specs = row_spec
    else:
        args += [modg, modg]
        specs += [mod_spec(l, sc_idx), mod_spec(l, sh_idx)]
        out_shape = [jax.ShapeDtypeStruct(x3.shape, f32), jax.ShapeDtypeStruct((ng * SUBLANES, D_MODEL), bf16)]
        out_specs = [row_spec, pl.BlockSpec((tb * SUBLANES, D_MODEL), lambda i: (i, 0))]
    return pl.pallas_call(
        functools.partial(_resnorm_kernel, has_delta=has_delta, final=final),
        grid=(ng // tb,), in_specs=specs, out_specs=out_specs, out_shape=out_shape,
        compiler_params=_cparams(("parallel",)),
        name="resnorm_final" if final else "resnorm",
    )(*args)


def _softplus(x):
    return jnp.maximum(x, 0.0) + jnp.log1p(jnp.exp(-jnp.abs(x)))


def _lora_kernel(*refs, vres):
    it = iter(refs)
    ps_ref, pp_ref, mu_ref, w0_ref, a0_ref, w2_ref, a2_ref, g2_ref = (next(it) for _ in range(8))
    if vres:
        pv_ref, pvp_ref, muv_ref, v0_ref, v1_ref, v2_ref = (next(it) for _ in range(6))
    dec_ref, a_ref, g_ref = next(it), next(it), next(it)
    ps = ps_ref[...]
    q = ps + (pp_ref[...] - ps) * mu_ref[...]
    o_a, o_g = RW_DECAY_LORA, RW_DECAY_LORA + RW_AAA_LORA
    wd, ad, gd = q[:, :o_a], q[:, o_a:o_g], q[:, o_g:o_g + GATE_LORA_PAD]
    lw = w0_ref[...] + jnp.dot(jnp.tanh(wd).astype(bf16), w2_ref[...], preferred_element_type=f32)
    logw = -_softplus(-lw) - 0.5
    dec_ref[...] = jnp.exp(-jnp.exp(logw))
    a_ref[...] = jax.nn.sigmoid(a0_ref[...] + jnp.dot(ad.astype(bf16), a2_ref[...], preferred_element_type=f32))
    g_ref[...] = jnp.dot(jax.nn.sigmoid(gd).astype(bf16), g2_ref[...], preferred_element_type=f32)
    if vres:
        vs_ref = next(it)
        pv = pv_ref[...]
        qv = pv + (pvp_ref[...] - pv) * muv_ref[...]
        vlo = jnp.dot(qv.astype(bf16), v1_ref[...], preferred_element_type=f32)
        vs_ref[...] = jax.nn.sigmoid(v0_ref[...] + jnp.dot(vlo.astype(bf16), v2_ref[...], preferred_element_type=f32))


def _rwkv_lora(p_small, prev_small, p_big, prev_v, lw, *, tm=128):
    rows = p_small.shape[0]
    tm = min(tm, rows)
    assert rows % tm == 0
    vres = lw["v1"] is not None
    row_s = pl.BlockSpec((tm, N_SMALL), lambda i: (i, 0))
    row_d = pl.BlockSpec((tm, D_MODEL), lambda i: (i, 0))

    def full(a):
        return pl.BlockSpec(a.shape, lambda i: (0,) * a.ndim)

    consts = [lw["mu_s"], lw["w0"], lw["a0"], lw["w2"], lw["a2"], lw["g2"]]
    args = [p_small, prev_small] + consts
    specs = [row_s, row_s] + [full(a) for a in consts]
    n_out = 3
    if vres:
        consts_v = [lw["mu_v"], lw["v0"], lw["v1"], lw["v2"]]
        args += [p_big, prev_v] + consts_v
        specs += [pl.BlockSpec((tm, D_MODEL), lambda i: (i, C_V // D_MODEL)), row_d] + [full(a) for a in consts_v]
        n_out = 4
    return pl.pallas_call(
        functools.partial(_lora_kernel, vres=vres),
        grid=(rows // tm,), in_specs=specs,
        out_specs=[row_d] * n_out,
        out_shape=[jax.ShapeDtypeStruct((rows, D_MODEL), f32)] * n_out,
        compiler_params=_cparams(("parallel",)),
        name="rwkv_lora",
    )(*args)


WKV_PAIRS = RW_HEAD // 2
(P_MU_R, P_MU_K, P_MU_V, P_KK, P_KA, P_RK, P_LNW, P_LNB) = range(8)


def _wkv_kernel(*refs, tc, nc, vres):
    it = iter(refs)
    pr_ref, pk_ref, pv_ref, w_ref, a_ref, g_ref = (next(it) for _ in range(6))
    if vres:
        vs_ref, vf_ref = next(it), next(it)
    init_ref, par_ref, s0_ref = next(it), next(it), next(it)
    y_ref = next(it)
    if not vres:
        vfo_ref = next(it)
    sout_ref = next(it)
    s_ref, carry_ref, r_s, k_s, v_s, a_s, b_s, y_s = (next(it) for _ in range(8))
    c = pl.program_id(1)

    @pl.when(c == 0)
    def _():
        s_ref[...] = s0_ref[...]
        carry_ref[...] = init_ref[...]

    def both(x):
        return x + pltpu.roll(x, RW_HEADS, axis=x.ndim - 1)

    def head_sum(x):
        return both(jnp.sum(x, axis=1, keepdims=True))

    def shifted(idx, ref):
        x = ref[...]
        prev = jnp.concatenate([carry_ref[idx][None], x[:-1]], axis=0)
        carry_ref[idx] = x[tc - 1]
        return x + (prev - x) * par_ref[idx]

    r = shifted(P_MU_R, pr_ref)
    k = shifted(P_MU_K, pk_ref)
    v = shifted(P_MU_V, pv_ref)
    if vres:
        v = v + (vf_ref[...] - v) * vs_ref[...]
    else:
        vfo_ref[...] = v
    a = a_ref[...]
    kk = k * par_ref[P_KK]
    kk = kk / jnp.maximum(jnp.sqrt(head_sum(kk * kk)), 1e-12)
    k = k * (1.0 + (a - 1.0) * par_ref[P_KA])
    r_s[...] = r
    k_s[...] = k
    v_s[...] = v
    a_s[...] = -kk
    b_s[...] = kk * a

    lane = lax.broadcasted_iota(jnp.int32, (WKV_PAIRS, LANES), 1)
    low = lane < RW_HEADS

    def step(t, carry):
        r_t, w_t, k_t, v_t, a_t, b_t = r_s[t], w_ref[t], k_s[t], v_s[t], a_s[t], b_s[t]
        wr = w_t * r_t
        sa = jnp.zeros((RW_HEAD, LANES), f32)
        yo = jnp.zeros((RW_HEAD, LANES), f32)
        for p in range(WKV_PAIRS):
            sp = s_ref[p]
            sa = sa + sp * a_t[p:p + 1, :]
            yo = yo + sp * wr[p:p + 1, :]
        sa = both(sa)
        yo = both(yo)
        br = both(jnp.sum(b_t * r_t, axis=0, keepdims=True))
        kr = both(jnp.sum(k_t * r_t, axis=0, keepdims=True))
        v_sw = pltpu.roll(v_t, RW_HEADS, axis=1)
        v_full = jnp.concatenate([jnp.where(low, v_t, v_sw), jnp.where(low, v_sw, v_t)], axis=0)
        for p in range(WKV_PAIRS):
            s_ref[p] = s_ref[p] * w_t[p:p + 1, :] + sa * b_t[p:p + 1, :] + v_full * k_t[p:p + 1, :]
        y = yo + sa * br + v_full * kr
        y_s[t] = jnp.where(low, y[:WKV_PAIRS], y[WKV_PAIRS:])
        return carry

    lax.fori_loop(0, tc, step, 0)

    y = y_s[...]
    mean = head_sum(y) * (1.0 / RW_HEAD)
    d = y - mean
    var = head_sum(d * d) * (1.0 / RW_HEAD)
    yn = d * lax.rsqrt(var + LNX_EPS) * par_ref[P_LNW] + par_ref[P_LNB]
    bonus = head_sum(r * k * par_ref[P_RK]) * v
    y_ref[...] = ((yn + bonus) * g_ref[...]).astype(y_ref.dtype)

    @pl.when(c == nc - 1)
    def _():
        sout_ref[...] = s_ref[...]


def _wkv(p_big3, dec3, a3, g3, vs3, vf3, init_prev, params, s0, *, row0, nseq, L, tc):
    assert L % tc == 0 and row0 % tc == 0
    nc = L // tc
    b0 = row0 // tc
    vres = vs3 is not None

    def pspec(col):
        return pl.BlockSpec((tc, WKV_PAIRS, LANES), lambda s, c: (b0 + s * nc + c, col // D_MODEL, 0))

    seq_spec = pl.BlockSpec((tc, WKV_PAIRS, LANES), lambda s, c: (b0 + s * nc + c, 0, 0))
    out_seq = pl.BlockSpec((tc, WKV_PAIRS, LANES), lambda s, c: (s * nc + c, 0, 0))
    st_spec = pl.BlockSpec((None, WKV_PAIRS, RW_HEAD, LANES), lambda s, c: (s, 0, 0, 0))
    args = [p_big3, p_big3, p_big3, dec3, a3, g3]
    specs = [pspec(C_R), pspec(C_K), pspec(C_V), seq_spec, seq_spec, seq_spec]
    if vres:
        args += [vs3, vf3]
        specs += [seq_spec, seq_spec]
    args += [init_prev, params, s0]
    specs += [pl.BlockSpec((None, 3, WKV_PAIRS, LANES), lambda s, c: (s, 0, 0, 0)),
              pl.BlockSpec(params.shape, lambda s, c: (0, 0, 0)), st_spec]
    n = nseq * L
    out_shape = [jax.ShapeDtypeStruct((n, WKV_PAIRS, LANES), bf16)]
    out_specs = [out_seq]
    if not vres:
        out_shape.append(jax.ShapeDtypeStruct((n, WKV_PAIRS, LANES), f32))
        out_specs.append(out_seq)
    out_shape.append(jax.ShapeDtypeStruct(s0.shape, f32))
    out_specs.append(st_spec)
    chunk = pltpu.VMEM((tc, WKV_PAIRS, LANES), f32)
    return pl.pallas_call(
        functools.partial(_wkv_kernel, tc=tc, nc=nc, vres=vres),
        grid=(nseq, nc), in_specs=specs, out_specs=out_specs, out_shape=out_shape,
        scratch_shapes=[pltpu.VMEM((WKV_PAIRS, RW_HEAD, LANES), f32), pltpu.VMEM((3, WKV_PAIRS, LANES), f32)] + [chunk] * 6,
        compiler_params=_cparams(("parallel", "arbitrary")),
        name="wkv7",
    )(*args)


def _wkv_state_to_kernel(s):
    n = s.shape[0]
    s = s.reshape(n, RW_HEADS, WKV_PAIRS, 2, WKV_PAIRS, 2)
    s = s.transpose(0, 4, 3, 2, 5, 1)
    return s.reshape(n, WKV_PAIRS, RW_HEAD, LANES)


def _wkv_state_from_kernel(s):
    n = s.shape[0]
    s = s.reshape(n, WKV_PAIRS, 2, WKV_PAIRS, 2, RW_HEADS)
    s = s.transpose(0, 5, 3, 2, 1, 4)
    return s.reshape(n, RW_HEADS, RW_HEAD, RW_HEAD)


CONV_HALO = SUBLANES


def _ssd_kernel(x_ref, b_ref, c_ref, z_ref, dt_ref, a_ref, wx_ref, wb_ref, wc_ref, bx_ref, bb_ref, bc_ref,
                d_ref, nw_ref, ix_ref, ib_ref, ic_ref, h0_ref, y_ref, hout_ref,
                h_ref, fx_ref, fb_ref, fc_ref, *, q, nc):
    c = pl.program_id(2)

    @pl.when(c == 0)
    def _():
        h_ref[...] = h0_ref[...].reshape(h_ref.shape)
        fx_ref[0:CONV_HALO, :] = ix_ref[...]
        fb_ref[0:CONV_HALO, :] = ib_ref[...]
        fc_ref[0:CONV_HALO, :] = ic_ref[...]

    def conv_silu(raw_ref, f_ref, w_ref, bias_ref):
        f_ref[CONV_HALO:CONV_HALO + q, :] = raw_ref[...]
        w = w_ref[...]
        acc = bias_ref[...]
        for i in range(MB_CONV):
            o = CONV_HALO - (MB_CONV - 1) + i
            acc = acc + f_ref[o:o + q, :] * w[i:i + 1, :]
        f_ref[0:CONV_HALO, :] = f_ref[q:q + CONV_HALO, :]
        return acc * jax.nn.sigmoid(acc)

    x = conv_silu(x_ref, fx_ref, wx_ref, bx_ref)
    bm = conv_silu(b_ref, fb_ref, wb_ref, bb_ref)
    cm = conv_silu(c_ref, fc_ref, wc_ref, bc_ref)

    nt = (((1,), (1,)), ((), ()))
    tn = (((0,), (0,)), ((), ()))
    hi = lax.Precision.HIGHEST
    row = lax.broadcasted_iota(jnp.int32, (q, q), 0)
    col = lax.broadcasted_iota(jnp.int32, (q, q), 1)
    causal = row >= col
    tri = causal.astype(f32)

    hg = h_ref[...]
    cg = cm.astype(bf16)
    bg = bm.astype(bf16)
    cb = lax.dot_general(cg, bg, nt, preferred_element_type=f32)
    y_state = lax.dot_general(cg, hg.astype(bf16), nt, preferred_element_type=f32)
    dt = dt_ref[...]
    dta = dt * a_ref[...]
    acum = jnp.dot(tri, dta, preferred_element_type=f32, precision=hi)
    acum_t = lax.dot_general(dta, tri, (((0,), (1,)), ((), ())), preferred_element_type=f32, precision=hi)
    ys, xts = [], []
    for r in range(MB_HPG):
        a_col = acum[:, r:r + 1]
        seg = jnp.where(causal, a_col - acum_t[r:r + 1, :], -jnp.inf)
        wts = cb * jnp.exp(seg)
        xdt = x[:, r * MB_HEAD:(r + 1) * MB_HEAD] * dt[:, r:r + 1]
        y_r = jnp.dot(wts.astype(bf16), xdt.astype(bf16), preferred_element_type=f32)
        y_r = y_r + y_state[:, r * MB_HEAD:(r + 1) * MB_HEAD] * jnp.exp(a_col)
        ys.append(y_r)
        xts.append(xdt * jnp.exp(acum[q - 1:q, r:r + 1] - a_col))
    xt = jnp.concatenate(xts, axis=1).astype(bf16)
    upd = lax.dot_general(xt, bg, tn, preferred_element_type=f32)
    for r in range(MB_HPG):
        sl = slice(r * MB_HEAD, (r + 1) * MB_HEAD)
        h_ref[sl, :] = hg[sl, :] * jnp.exp(acum[q - 1:q, r:r + 1]) + upd[sl, :]

    y = jnp.concatenate(ys, axis=1) + x * d_ref[...]
    z = z_ref[...]
    gated = y * (z * jax.nn.sigmoid(z))
    gated = gated * lax.rsqrt(jnp.mean(gated * gated, axis=-1, keepdims=True) + GATED_NORM_EPS)
    y_ref[...] = (gated * nw_ref[...]).astype(y_ref.dtype)

    @pl.when(c == nc - 1)
    def _():
        hout_ref[...] = h_ref[...].reshape(hout_ref.shape)


def _ssd(p_big, dtg, ag, lw, conv_init, h0, *, row0, nseq, L, q):
    assert L % q == 0 and row0 % q == 0
    nc = L // q
    b0 = row0 // q
    gw = MB_HPG * MB_HEAD
    xg, bgc, cgc, zg = C_X // gw, C_B // MB_STATE, C_C // MB_STATE, C_Z // gw
    ib, ic = MB_INNER // MB_STATE, (MB_INNER + MB_BC) // MB_STATE

    def rows(width, col0):
        return pl.BlockSpec((q, width), lambda s, g, c: (b0 + s * nc + c, col0 + g))

    def per_group(nrows, width, col0=0):
        return pl.BlockSpec((nrows, width), lambda s, g, c: (0, col0 + g))

    st_spec = pl.BlockSpec((None, MB_HPG, MB_HEAD, MB_STATE), lambda s, g, c: (s, g, 0, 0))
    in_specs = [rows(gw, xg), rows(MB_STATE, bgc), rows(MB_STATE, cgc), rows(gw, zg),
                pl.BlockSpec((None, q, MB_HPG), lambda s, g, c: (g, b0 + s * nc + c, 0)),
                pl.BlockSpec((None, 1, MB_HPG), lambda s, g, c: (g, 0, 0)),
                per_group(MB_CONV, gw), per_group(MB_CONV, MB_STATE, ib), per_group(MB_CONV, MB_STATE, ic),
                per_group(1, gw), per_group(1, MB_STATE, ib), per_group(1, MB_STATE, ic),
                per_group(1, gw), per_group(1, gw),
                pl.BlockSpec((None, CONV_HALO, gw), lambda s, g, c: (s, 0, g)),
                pl.BlockSpec((None, CONV_HALO, MB_STATE), lambda s, g, c: (s, 0, ib + g)),
                pl.BlockSpec((None, CONV_HALO, MB_STATE), lambda s, g, c: (s, 0, ic + g)),
                st_spec]
    cw, cbias = lw["conv_w"], lw["conv_b"]
    return pl.pallas_call(
        functools.partial(_ssd_kernel, q=q, nc=nc),
        grid=(nseq, MB_GROUPS, nc),
        in_specs=in_specs,
        out_specs=[pl.BlockSpec((q, gw), lambda s, g, c: (s * nc + c, g)), st_spec],
        out_shape=[jax.ShapeDtypeStruct((nseq * L, MB_INNER), bf16), jax.ShapeDtypeStruct(h0.shape, f32)],
        scratch_shapes=[pltpu.VMEM((gw, MB_STATE), f32), pltpu.VMEM((q + CONV_HALO, gw), f32),
                        pltpu.VMEM((q + CONV_HALO, MB_STATE), f32), pltpu.VMEM((q + CONV_HALO, MB_STATE), f32)],
        compiler_params=_cparams(("parallel", "parallel", "arbitrary")),
        name="ssd",
    )(p_big, p_big, p_big, p_big, dtg, ag, cw, cw, cw, cbias, cbias, cbias, lw["d"], lw["norm"],
      conv_init, conv_init, conv_init, h0)


def _perm_last(w):
    s = w.shape[:-1]
    return w.reshape(s + (RW_HEADS, RW_HEAD)).swapaxes(-1, -2).reshape(s + (D_MODEL,))


def _perm_rows(w):
    return w.reshape((RW_HEADS, RW_HEAD) + w.shape[1:]).swapaxes(0, 1).reshape(w.shape)


def _pad_last(w, n):
    return jnp.pad(w, [(0, 0)] * (w.ndim - 1) + [(0, n - w.shape[-1])])


def _perm_rkv(w):
    s = w.shape[:-1]
    return _perm_last(w.reshape(s + (3, D_MODEL))).reshape(s + (3 * D_MODEL,))


def _split_in_cols(w):
    rkv = _perm_rkv(w[..., :3 * D_MODEL])
    lora = w[..., 3 * D_MODEL:RW_COLS]
    mb = w[..., RW_COLS:RW_COLS + MB_COLS]
    z, xbc, dt = mb[..., :MB_INNER], mb[..., MB_INNER:MB_INNER + MB_CONV_DIM], mb[..., MB_INNER + MB_CONV_DIM:]
    gates = w[..., RW_COLS + MB_COLS:]
    big = jnp.concatenate([rkv, gates, z, xbc], axis=-1)
    small = jnp.concatenate([_pad_last(lora, S_LORA_PAD), _pad_last(dt, N_SMALL - S_LORA_PAD)], axis=-1)
    return big, small


def _rwkv_layer_weights(l, rw_mu, rw_w0, rw_w2, rw_a0, rw_a2, rw_g2, rw_kk, rw_ka, rw_rk, rw_lnx_w, rw_lnx_b,
                        rw_v0, rw_v1, rw_v2):
    def tile(v):
        return _perm_last(v).reshape(WKV_PAIRS, LANES)

    mu = rw_mu[l]
    params = jnp.stack([tile(mu[C_R:C_R + D_MODEL]), tile(mu[C_K:C_K + D_MODEL]), tile(mu[C_V:C_V + D_MODEL]),
                        tile(rw_kk[l]), tile(rw_ka[l]), rw_rk[l].T.reshape(WKV_PAIRS, LANES),
                        tile(rw_lnx_w[l]), tile(rw_lnx_b[l])])
    lw = dict(
        params=params,
        mu_s=_pad_last(mu[3 * D_MODEL:], N_SMALL).reshape(1, N_SMALL),
        w0=_perm_last(rw_w0[l]).reshape(1, D_MODEL), a0=_perm_last(rw_a0[l]).reshape(1, D_MODEL),
        w2=_perm_last(rw_w2[l]).astype(bf16), a2=_perm_last(rw_a2[l]).astype(bf16),
        g2=jnp.pad(_perm_last(rw_g2[l]), ((0, GATE_LORA_PAD - RW_GATE_LORA), (0, 0))).astype(bf16),
        mu_v=None, v0=None, v1=None, v2=None)
    if l > 0:
        lw.update(
            mu_v=_perm_last(mu[C_V:C_V + D_MODEL]).reshape(1, D_MODEL),
            v0=_perm_last(rw_v0[l - 1]).reshape(1, D_MODEL),
            v1=_pad_last(_perm_rows(rw_v1[l - 1]), MV_LORA_PAD).astype(bf16),
            v2=jnp.pad(_perm_last(rw_v2[l - 1]), ((0, MV_LORA_PAD - RW_MV_LORA), (0, 0))).astype(bf16))
    return lw


def _mamba_layer_weights(l, mb_conv_w, mb_conv_b, mb_dt_bias, mb_a_log, mb_d, mb_norm):
    return dict(conv_w=mb_conv_w[l], conv_b=mb_conv_b[l].reshape(1, MB_CONV_DIM), dt_bias=mb_dt_bias[l],
                a_log=mb_a_log[l], d=jnp.repeat(mb_d[l], MB_HEAD).reshape(1, MB_INNER),
                norm=mb_norm[l].reshape(1, MB_INNER))


def _prev_rows(t, groups, inits):
    out, off = [], 0
    for (nseq, L), init in zip(groups, inits):
        tt = t[off:off + nseq * L].reshape(nseq, L, t.shape[-1])
        out.append(jnp.concatenate([init[:, None, :], tt[:, :-1]], axis=1).reshape(nseq * L, t.shape[-1]))
        off += nseq * L
    return jnp.concatenate(out, axis=0)


def _last_rows(t, groups):
    out, off = [], 0
    for nseq, L in groups:
        out.append(t[off:off + nseq * L].reshape(nseq, L, t.shape[-1])[:, -1])
        off += nseq * L
    return out


def _rwkv_branch(p_big, p_small, shift_prev, wkv_prev, v_first, lw, groups):
    rows = p_big.shape[0]
    vres = lw["v1"] is not None
    sp_rkv = [_perm_rkv(s[:, :3 * D_MODEL]) for s in shift_prev]
    sp_small = [_pad_last(s[:, 3 * D_MODEL:], N_SMALL) for s in shift_prev]
    prev_small = _prev_rows(p_small, groups, sp_small)
    prev_v = _prev_rows(p_big[:, C_V:C_V + D_MODEL], groups, [s[:, C_V:] for s in sp_rkv]) if vres else None
    lora = _rwkv_lora(p_small, prev_small, p_big, prev_v, lw)
    to3 = lambda t: t.reshape(rows, WKV_PAIRS, LANES)
    dec3, a3, g3 = to3(lora[0]), to3(lora[1]), to3(lora[2])
    vs3 = to3(lora[3]) if vres else None
    p_big3 = p_big.reshape(rows, N_BIG // LANES, LANES)
    ys, vfs, s_new, off = [], [], [], 0
    for (nseq, L), s0, sp in zip(groups, wkv_prev, sp_rkv):
        outs = _wkv(p_big3, dec3, a3, g3, vs3, v_first, sp.reshape(nseq, 3, WKV_PAIRS, LANES), lw["params"], s0,
                    row0=off, nseq=nseq, L=L, tc=min(L, 64))
        ys.append(outs[0])
        if not vres:
            vfs.append(outs[1])
        s_new.append(outs[-1])
        off += nseq * L
    y = jnp.concatenate(ys, axis=0).reshape(rows, D_MODEL)
    if not vres:
        v_first = jnp.concatenate(vfs, axis=0)
    shift_new = [jnp.concatenate([_perm_rkv(a), b[:, :RW_LORA]], axis=-1)
                 for a, b in zip(_last_rows(p_big[:, :3 * D_MODEL], groups), _last_rows(p_small, groups))]
    return y, shift_new, s_new, v_first


def _mamba_branch(p_big, p_small, conv_prev, ssm_prev, lw, groups):
    rows = p_big.shape[0]
    dt = jax.nn.softplus(p_small[:, S_DT:S_DT + MB_HEADS] + lw["dt_bias"])
    dtg = dt.reshape(rows, MB_GROUPS, MB_HPG).transpose(1, 0, 2)
    ag = (-jnp.exp(lw["a_log"])).reshape(MB_GROUPS, 1, MB_HPG)
    ys, ssm_new, conv_new, off = [], [], [], 0
    for (nseq, L), cp, h0 in zip(groups, conv_prev, ssm_prev):
        conv_init = jnp.pad(cp, ((0, 0), (CONV_HALO - (MB_CONV - 1), 0), (0, 0)))
        y_g, h_g = _ssd(p_big, dtg, ag, lw, conv_init, h0, row0=off, nseq=nseq, L=L, q=min(L, 128))
        ys.append(y_g)
        ssm_new.append(h_g)
        xbc = p_big[off:off + nseq * L, C_X:C_X + MB_CONV_DIM].reshape(nseq, L, MB_CONV_DIM)
        conv_new.append(jnp.concatenate([cp, xbc], axis=1)[:, L:] if L < MB_CONV - 1 else xbc[:, L - (MB_CONV - 1):])
        off += nseq * L
    return jnp.concatenate(ys, axis=0), conv_new, ssm_new


def kernel(x_prompt, x_sample, c_prompt, c_sample, state_rwkv_shift, state_rwkv_wkv, state_mamba_conv, state_mamba_ssm, ada_w, ada_b, norm1, norm2, w_in, rw_mu, rw_w0, rw_w2, rw_a0, rw_a2, rw_g2, rw_kk, rw_ka, rw_rk, rw_lnx_w, rw_lnx_b, rw_v0, rw_v1, rw_v2, mb_conv_w, mb_conv_b, mb_dt_bias, mb_a_log, mb_d, mb_norm, w_proj_a, w_proj_b, w_out, ffn_gate, ffn_up, ffn_down, norm_f):
    bp, lp, _ = x_prompt.shape
    bs, ls, _ = x_sample.shape
    groups = ((bp, lp), (bs, ls))
    rows = bp * lp + bs * ls
    assert lp % SUBLANES == 0 and ls == SUBLANES

    nc_rows = bp + bs
    c_pad = -(-nc_rows // 16) * 16
    c_all = jnp.pad(jnp.concatenate([c_prompt, c_sample], axis=0), ((0, c_pad - nc_rows), (0, 0)))
    mod = _mod_all(c_all, ada_w, ada_b)[:, :nc_rows].reshape(DEPTH, nc_rows, N_MOD, 1, D_MODEL)
    modg = jnp.concatenate([jnp.repeat(mod[:, :bp], lp // SUBLANES, axis=1), mod[:, bp:]], axis=1)

    x3 = jnp.concatenate([x_prompt.reshape(-1, D_MODEL), x_sample.reshape(-1, D_MODEL)], axis=0)
    x3 = x3.reshape(rows // SUBLANES, SUBLANES, D_MODEL)

    zeros = functools.partial(jnp.zeros, dtype=f32)
    delta3, v_first = None, None
    shifts, wkvs, convs, ssms = [], [], [], []
    for l in range(DEPTH):
        w_big, w_small = _split_in_cols(w_in[l].astype(bf16))
        lw_rw = _rwkv_layer_weights(l, rw_mu, rw_w0, rw_w2, rw_a0, rw_a2, rw_g2, rw_kk, rw_ka, rw_rk, rw_lnx_w,
                                    rw_lnx_b, rw_v0, rw_v1, rw_v2)
        lw_mb = _mamba_layer_weights(l, mb_conv_w, mb_conv_b, mb_dt_bias, mb_a_log, mb_d, mb_norm)

        x3, h = _resnorm(x3, delta3, modg, (l - 1, 5), norm1[l], l, 1, 0)
        p_big = _mm(h, w_big, sigmoid_tiles=(C_GATE // 1024, C_Z // 1024), name="in_big")
        p_small = _mm(h, w_small, name="in_small")

        shift_prev = [zeros((bp, RW_COLS)), state_rwkv_shift[l]]
        wkv_prev = [zeros((bp, WKV_PAIRS, RW_HEAD, LANES)), _wkv_state_to_kernel(state_rwkv_wkv[l])]
        y_a, sh_new, wkv_new, v_first = _rwkv_branch(p_big, p_small, shift_prev, wkv_prev, v_first, lw_rw, groups)
        conv_prev = [zeros((bp, MB_CONV - 1, MB_CONV_DIM)), state_mamba_conv[l]]
        ssm_prev = [zeros((bp, MB_HEADS, MB_HEAD, MB_STATE)), state_mamba_ssm[l]]
        y_b, conv_new, ssm_new = _mamba_branch(p_big, p_small, conv_prev, ssm_prev, lw_mb, groups)

        merged = _merge_mm(y_a, _perm_rows(w_proj_a[l].astype(bf16)), y_b, w_proj_b[l].astype(bf16), p_big)
        mo = _mm(merged, w_out[l].astype(bf16), name="w_out")
        x3, h2 = _resnorm(x3, mo.reshape(x3.shape), modg, (l, 2), norm2[l], l, 4, 3)
        hid = _swiglu_mm(h2, _pad_last(ffn_gate[l].astype(bf16), D_FF_PAD), _pad_last(ffn_up[l].astype(bf16), D_FF_PAD))
        ff = _mm(hid, jnp.pad(ffn_down[l].astype(bf16), ((0, D_FF_PAD - D_FF), (0, 0))), tk=1024, name="ffn_down")
        delta3 = ff.reshape(x3.shape)

        shifts.append(sh_new)
        wkvs.append([_wkv_state_from_kernel(s) for s in wkv_new])
        convs.append(conv_new)
        ssms.append(ssm_new)

    y = _resnorm(x3, delta3, modg, (DEPTH - 1, 5), norm_f, None, None, None, final=True).reshape(rows, D_MODEL)
    y_prompt = y[:bp * lp].reshape(bp, lp, D_MODEL)
    y_sample = y[bp * lp:].reshape(bs, ls, D_MODEL)

    def stack(lst, gi):
        return jnp.stack([t[gi] for t in lst])

    return (y_prompt, y_sample,
            stack(shifts, 0), stack(wkvs, 0), stack(convs, 0), stack(ssms, 0),
            stack(shifts, 1), stack(wkvs, 1), stack(convs, 1), stack(ssms, 1))
```

```python
import functools

import jax
import jax.numpy as jnp
import numpy as np
from jax import lax
from jax.experimental import pallas as pl
from jax.experimental.pallas import tpu as pltpu

f32 = jnp.float32
bf16 = jnp.bfloat16

D_MODEL = 4096
DEPTH = 4
RW_HEAD = 64
RW_HEADS = D_MODEL // RW_HEAD
RW_DECAY_LORA = 128
RW_AAA_LORA = 128
RW_MV_LORA = 96
RW_GATE_LORA = 480
RW_LORA = RW_DECAY_LORA + RW_AAA_LORA + RW_GATE_LORA
RW_COLS = 3 * D_MODEL + RW_LORA
LNX_EPS = 64e-5
MB_INNER = 2 * D_MODEL
MB_HEAD = 64
MB_HEADS = MB_INNER // MB_HEAD
MB_GROUPS = 8
MB_HPG = MB_HEADS // MB_GROUPS
MB_STATE = 128
MB_CONV = 4
MB_BC = MB_GROUPS * MB_STATE
MB_CONV_DIM = MB_INNER + 2 * MB_BC
MB_COLS = MB_INNER + MB_CONV_DIM + MB_HEADS
GATE_COLS = 2 * D_MODEL
D_FF = 11008
N_MOD = 6
NORM_EPS = 1e-6
GATED_NORM_EPS = 1e-5

LANES = 128
SUBLANES = 8
GATE_LORA_PAD = 512
MV_LORA_PAD = 128
D_FF_PAD = 11264
VMEM_LIMIT = 52 * 1024 * 1024

C_R, C_K, C_V = 0, D_MODEL, 2 * D_MODEL
N_RKV = 3 * D_MODEL
M_Z = 0
M_X = MB_INNER
M_B = M_X + MB_INNER
M_C = M_B + MB_BC
N_MB = M_C + MB_BC
S_LORA_PAD = 768
S_DT = S_LORA_PAD
N_SMALL = 1024


def _cparams(sem):
    return pltpu.CompilerParams(dimension_semantics=sem, vmem_limit_bytes=VMEM_LIMIT)


def _mm_kernel(a_ref, b_ref, o_ref, acc_ref, *, nk, sigmoid):
    k = pl.program_id(2)

    @pl.when(k == 0)
    def _():
        acc_ref[...] = jnp.zeros_like(acc_ref)

    acc_ref[...] += jnp.dot(a_ref[...], b_ref[...], preferred_element_type=f32)

    @pl.when(k == nk - 1)
    def _():
        r = acc_ref[...]
        if sigmoid:
            r = jax.nn.sigmoid(r)
        o_ref[...] = r.astype(o_ref.dtype)


def _wspec(tk, tn, layer, index):
    return pl.BlockSpec((None, tk, tn), lambda i, j, k: (layer,) + index(i, j, k))


def _mm(a, b, layer, *, tm=1024, tn=1024, tk=2048, out_dtype=f32, sigmoid=False, name="mm"):
    m, kd = a.shape
    n = b.shape[-1]
    tm, tn, tk = min(tm, m), min(tn, n), min(tk, kd)
    assert m % tm == 0 and n % tn == 0 and kd % tk == 0, (a.shape, b.shape, tm, tn, tk)
    nk = kd // tk
    return pl.pallas_call(
        functools.partial(_mm_kernel, nk=nk, sigmoid=sigmoid),
        grid=(m // tm, n // tn, nk),
        in_specs=[pl.BlockSpec((tm, tk), lambda i, j, k: (i, k)),
                  _wspec(tk, tn, layer, lambda i, j, k: (k, j))],
        out_specs=pl.BlockSpec((tm, tn), lambda i, j, k: (i, j)),
        out_shape=jax.ShapeDtypeStruct((m, n), out_dtype),
        scratch_shapes=[pltpu.VMEM((tm, tn), f32)],
        compiler_params=_cparams(("parallel", "parallel", "arbitrary")),
        name=name,
    )(a, b)


def _swiglu_kernel(a_ref, bg_ref, bu_ref, o_ref, accg_ref, accu_ref, *, nk):
    k = pl.program_id(2)

    @pl.when(k == 0)
    def _():
        accg_ref[...] = jnp.zeros_like(accg_ref)
        accu_ref[...] = jnp.zeros_like(accu_ref)

    a = a_ref[...]
    accg_ref[...] += jnp.dot(a, bg_ref[...], preferred_element_type=f32)
    accu_ref[...] += jnp.dot(a, bu_ref[...], preferred_element_type=f32)

    @pl.when(k == nk - 1)
    def _():
        g = accg_ref[...]
        o_ref[...] = (g * jax.nn.sigmoid(g) * accu_ref[...]).astype(o_ref.dtype)


def _swiglu_mm(a, bg, bu, layer, *, tm=1024, tn=1024, tk=2048):
    m, kd = a.shape
    n = bg.shape[-1]
    tm = min(tm, m)
    assert m % tm == 0 and n % tn == 0 and kd % tk == 0
    nk = kd // tk
    return pl.pallas_call(
        functools.partial(_swiglu_kernel, nk=nk),
        grid=(m // tm, n // tn, nk),
        in_specs=[pl.BlockSpec((tm, tk), lambda i, j, k: (i, k)),
                  _wspec(tk, tn, layer, lambda i, j, k: (k, j)),
                  _wspec(tk, tn, layer, lambda i, j, k: (k, j))],
        out_specs=pl.BlockSpec((tm, tn), lambda i, j, k: (i, j)),
        out_shape=jax.ShapeDtypeStruct((m, n), bf16),
        scratch_shapes=[pltpu.VMEM((tm, tn), f32), pltpu.VMEM((tm, tn), f32)],
        compiler_params=_cparams(("parallel", "parallel", "arbitrary")),
        name="swiglu_mm",
    )(a, bg, bu)


def _merge_kernel(ya_ref, wa_ref, yb_ref, wb_ref, ga_ref, gb_ref, o_ref, acca_ref, accb_ref, *, nka, nkb):
    k = pl.program_id(2)

    @pl.when(k == 0)
    def _():
        acca_ref[...] = jnp.zeros_like(acca_ref)
        accb_ref[...] = jnp.zeros_like(accb_ref)

    @pl.when(k < nka)
    def _():
        acca_ref[...] += jnp.dot(ya_ref[...], wa_ref[...], preferred_element_type=f32)

    @pl.when(k >= nka)
    def _():
        accb_ref[...] += jnp.dot(yb_ref[...], wb_ref[...], preferred_element_type=f32)

    @pl.when(k == nka + nkb - 1)
    def _():
        o_ref[...] = (ga_ref[...] * acca_ref[...] + gb_ref[...] * accb_ref[...]).astype(o_ref.dtype)


def _merge_mm(ya, wa, yb, wb, layer, p_gate, *, tm=1024, tn=1024, tk=1024):
    m, ka = ya.shape
    kb = yb.shape[1]
    n = wa.shape[-1]
    tm = min(tm, m)
    assert m % tm == 0 and n % tn == 0 and ka % tk == 0 and kb % tk == 0
    nka, nkb = ka // tk, kb // tk
    ga0, gb0 = 0, n // tn
    return pl.pallas_call(
        functools.partial(_merge_kernel, nka=nka, nkb=nkb),
        grid=(m // tm, n // tn, nka + nkb),
        in_specs=[pl.BlockSpec((tm, tk), lambda i, j, k: (i, jnp.minimum(k, nka - 1))),
                  _wspec(tk, tn, layer, lambda i, j, k: (jnp.minimum(k, nka - 1), j)),
                  pl.BlockSpec((tm, tk), lambda i, j, k: (i, jnp.maximum(k - nka, 0))),
                  _wspec(tk, tn, layer, lambda i, j, k: (jnp.maximum(k - nka, 0), j)),
                  pl.BlockSpec((tm, tn), lambda i, j, k: (i, ga0 + j)),
                  pl.BlockSpec((tm, tn), lambda i, j, k: (i, gb0 + j))],
        out_specs=pl.BlockSpec((tm, tn), lambda i, j, k: (i, j)),
        out_shape=jax.ShapeDtypeStruct((m, n), bf16),
        scratch_shapes=[pltpu.VMEM((tm, tn), f32), pltpu.VMEM((tm, tn), f32)],
        compiler_params=_cparams(("parallel", "parallel", "arbitrary")),
        name="merge_mm",
    )(ya, wa, yb, wb, p_gate, p_gate)


def _mod_kernel(c_ref, w_ref, b_ref, o_ref):
    c = c_ref[...]
    a = (c * jax.nn.sigmoid(c)).astype(bf16)
    o_ref[...] = jnp.dot(a, w_ref[...].astype(bf16), preferred_element_type=f32) + b_ref[...]


def _mod_all(c_all, ada_w, ada_b, *, tn=512):
    rows = c_all.shape[0]
    n = N_MOD * D_MODEL
    return pl.pallas_call(
        _mod_kernel,
        grid=(DEPTH, n // tn),
        in_specs=[pl.BlockSpec((rows, D_MODEL), lambda l, j: (0, 0)),
                  pl.BlockSpec((None, D_MODEL, tn), lambda l, j: (l, 0, j)),
                  pl.BlockSpec((None, 1, tn), lambda l, j: (l, 0, j))],
        out_specs=pl.BlockSpec((None, rows, tn), lambda l, j: (l, 0, j)),
        out_shape=jax.ShapeDtypeStruct((DEPTH, rows, n), f32),
        compiler_params=_cparams(("parallel", "parallel")),
        name="adaln_mod",
    )(c_all, ada_w, ada_b.reshape(DEPTH, 1, n))


def _resnorm_kernel(*refs, has_delta, final):
    it = iter(refs)
    x_ref = next(it)
    if has_delta:
        d_ref, gate_ref = next(it), next(it)
    g_ref = next(it)
    if not final:
        sc_ref, sh_ref = next(it), next(it)
    x = x_ref[...]
    if has_delta:
        x = x + gate_ref[...] * d_ref[...]
    y = x * lax.rsqrt(jnp.mean(x * x, axis=-1, keepdims=True) + NORM_EPS) * g_ref[...]
    if final:
        o_ref = next(it)
        o_ref[...] = y
    else:
        xo_ref, h_ref = next(it), next(it)
        xo_ref[...] = x
        h = y * (1.0 + sc_ref[...]) + sh_ref[...]
        h_ref[...] = h.reshape(h_ref.shape).astype(bf16)


def _resnorm(x3, delta3, modg, gate_at, g, l, sc_idx, sh_idx, *, final=False, tb=16):
    ng = x3.shape[0]
    assert ng % tb == 0
    has_delta = delta3 is not None
    row_spec = pl.BlockSpec((tb, SUBLANES, D_MODEL), lambda i: (i, 0, 0))

    def mod_spec(layer, idx):
        return pl.BlockSpec((None, tb, None, 1, D_MODEL), lambda i: (layer, i, idx, 0, 0))

    args, specs = [x3], [row_spec]
    if has_delta:
        args += [delta3, modg]
        specs += [row_spec, mod_spec(*gate_at)]
    args.append(g.reshape(1, 1, D_MODEL))
    specs.append(pl.BlockSpec((1, 1, D_MODEL), lambda i: (0, 0, 0)))
    if final:
        out_shape = jax.ShapeDtypeStruct(x3.shape, f32)
        out_specs = row_spec
    else:
        args += [modg, modg]
        specs += [mod_spec(l, sc_idx), mod_spec(l, sh_idx)]
        out_shape = [jax.ShapeDtypeStruct(x3.shape, f32), jax.ShapeDtypeStruct((ng * SUBLANES, D_MODEL), bf16)]
        out_specs = [row_spec, pl.BlockSpec((tb * SUBLANES, D_MODEL), lambda i: (i, 0))]
    return pl.pallas_call(
        functools.partial(_resnorm_kernel, has_delta=has_delta, final=final),
        grid=(ng // tb,), in_specs=specs, out_specs=out_specs, out_shape=out_shape,
        compiler_params=_cparams(("parallel",)),
        name="resnorm_final" if final else "resnorm",
    )(*args)


def _softplus(x):
    return jnp.maximum(x, 0.0) + jnp.log1p(jnp.exp(-jnp.abs(x)))


def _lora_kernel(*refs, vres):
    it = iter(refs)
    ps_ref, pp_ref, mu_ref, w0_ref, a0_ref, w2_ref, a2_ref, g2_ref = (next(it) for _ in range(8))
    if vres:
        pv_ref, pvp_ref, muv_ref, v0_ref, v1_ref, v2_ref = (next(it) for _ in range(6))
    dec_ref, a_ref, g_ref = next(it), next(it), next(it)
    ps = ps_ref[...]
    q = ps + (pp_ref[...] - ps) * mu_ref[...]
    o_a, o_g = RW_DECAY_LORA, RW_DECAY_LORA + RW_AAA_LORA
    wd, ad, gd = q[:, :o_a], q[:, o_a:o_g], q[:, o_g:o_g + GATE_LORA_PAD]
    lw = w0_ref[...] + jnp.dot(jnp.tanh(wd).astype(bf16), w2_ref[...], preferred_element_type=f32)
    logw = -_softplus(-lw) - 0.5
    dec_ref[...] = jnp.exp(-jnp.exp(logw))
    a_ref[...] = jax.nn.sigmoid(a0_ref[...] + jnp.dot(ad.astype(bf16), a2_ref[...], preferred_element_type=f32))
    g_ref[...] = jnp.dot(jax.nn.sigmoid(gd).astype(bf16), g2_ref[...], preferred_element_type=f32)
    if vres:
        vs_ref = next(it)
        pv = pv_ref[...]
        qv = pv + (pvp_ref[...] - pv) * muv_ref[...]
        vlo = jnp.dot(qv.astype(bf16), v1_ref[...], preferred_element_type=f32)
        vs_ref[...] = jax.nn.sigmoid(v0_ref[...] + jnp.dot(vlo.astype(bf16), v2_ref[...], preferred_element_type=f32))


def _rwkv_lora(p_small, prev_small, p_rkv, prev_v, lw, *, tm=128):
    rows = p_small.shape[0]
    tm = min(tm, rows)
    assert rows % tm == 0
    vres = lw["v1"] is not None
    row_s = pl.BlockSpec((tm, N_SMALL), lambda i: (i, 0))
    row_d = pl.BlockSpec((tm, D_MODEL), lambda i: (i, 0))

    def full(a):
        return pl.BlockSpec(a.shape, lambda i: (0,) * a.ndim)

    consts = [lw["mu_s"], lw["w0"], lw["a0"], lw["w2"], lw["a2"], lw["g2"]]
    args = [p_small, prev_small] + consts
    specs = [row_s, row_s] + [full(a) for a in consts]
    n_out = 3
    if vres:
        consts_v = [lw["mu_v"], lw["v0"], lw["v1"], lw["v2"]]
        args += [p_rkv, prev_v] + consts_v
        specs += [pl.BlockSpec((tm, D_MODEL), lambda i: (i, C_V // D_MODEL)), row_d] + [full(a) for a in consts_v]
        n_out = 4
    return pl.pallas_call(
        functools.partial(_lora_kernel, vres=vres),
        grid=(rows // tm,), in_specs=specs,
        out_specs=[row_d] * n_out,
        out_shape=[jax.ShapeDtypeStruct((rows, D_MODEL), f32)] * n_out,
        compiler_params=_cparams(("parallel",)),
        name="rwkv_lora",
    )(*args)


WKV_PAIRS = RW_HEAD // 2
(P_MU_R, P_MU_K, P_MU_V, P_KK, P_KA, P_RK, P_LNW, P_LNB) = range(8)


def _wkv_kernel(*refs, tc, nc, vres, has_acc):
    it = iter(refs)
    pr_ref, pk_ref, pv_ref, w_ref, a_ref, g_ref = (next(it) for _ in range(6))
    if vres:
        vs_ref, vf_ref = next(it), next(it)
    init_ref, par_ref, s0_ref = next(it), next(it), next(it)
    if has_acc:
        next(it)
    y_ref = next(it)
    if not vres:
        vfo_ref = next(it)
    sout_ref = next(it)
    s_ref, carry_ref, r_s, k_s, v_s, a_s, b_s, y_s = (next(it) for _ in range(8))
    c = pl.program_id(1)

    @pl.when(c == 0)
    def _():
        s_ref[...] = s0_ref[...]
        carry_ref[...] = init_ref[...]

    def both(x):
        return x + pltpu.roll(x, RW_HEADS, axis=x.ndim - 1)

    def head_sum(x):
        return both(jnp.sum(x, axis=1, keepdims=True))

    def shifted(idx, ref):
        x = ref[...]
        prev = jnp.concatenate([carry_ref[idx][None], x[:-1]], axis=0)
        carry_ref[idx] = x[tc - 1]
        return x + (prev - x) * par_ref[idx]

    r = shifted(P_MU_R, pr_ref)
    k = shifted(P_MU_K, pk_ref)
    v = shifted(P_MU_V, pv_ref)
    if vres:
        v = v + (vf_ref[...] - v) * vs_ref[...]
    else:
        vfo_ref[...] = v
    a = a_ref[...]
    kk = k * par_ref[P_KK]
    kk = kk / jnp.maximum(jnp.sqrt(head_sum(kk * kk)), 1e-12)
    k = k * (1.0 + (a - 1.0) * par_ref[P_KA])
    r_s[...] = r
    k_s[...] = k
    v_s[...] = v
    a_s[...] = -kk
    b_s[...] = kk * a

    lane = lax.broadcasted_iota(jnp.int32, (WKV_PAIRS, LANES), 1)
    low = lane < RW_HEADS

    def step(t, carry):
        r_t, w_t, k_t, v_t, a_t, b_t = r_s[t], w_ref[t], k_s[t], v_s[t], a_s[t], b_s[t]
        wr = w_t * r_t
        sa = jnp.zeros((RW_HEAD, LANES), f32)
        yo = jnp.zeros((RW_HEAD, LANES), f32)
        for p in range(WKV_PAIRS):
            sp = s_ref[p]
            sa = sa + sp * a_t[p:p + 1, :]
            yo = yo + sp * wr[p:p + 1, :]
        sa = both(sa)
        yo = both(yo)
        br = both(jnp.sum(b_t * r_t, axis=0, keepdims=True))
        kr = both(jnp.sum(k_t * r_t, axis=0, keepdims=True))
        v_sw = pltpu.roll(v_t, RW_HEADS, axis=1)
        v_full = jnp.concatenate([jnp.where(low, v_t, v_sw), jnp.where(low, v_sw, v_t)], axis=0)
        for p in range(WKV_PAIRS):
            s_ref[p] = s_ref[p] * w_t[p:p + 1, :] + sa * b_t[p:p + 1, :] + v_full * k_t[p:p + 1, :]
        y = yo + sa * br + v_full * kr
        y_s[t] = jnp.where(low, y[:WKV_PAIRS], y[WKV_PAIRS:])
        return carry

    lax.fori_loop(0, tc, step, 0)

    y = y_s[...]
    mean = head_sum(y) * (1.0 / RW_HEAD)
    d = y - mean
    var = head_sum(d * d) * (1.0 / RW_HEAD)
    yn = d * lax.rsqrt(var + LNX_EPS) * par_ref[P_LNW] + par_ref[P_LNB]
    bonus = head_sum(r * k * par_ref[P_RK]) * v
    y_ref[...] = ((yn + bonus) * g_ref[...]).astype(y_ref.dtype)

    @pl.when(c == nc - 1)
    def _():
        sout_ref[...] = s_ref[...]


def _state_io(st, blk, nseq, ndim_grid):
    lay_in, lay_out = st["layer"], st["out_layer"]
    in_spec = pl.BlockSpec((None, None) + blk, lambda *ids: (lay_in, ids[0]) + _state_tail(ids, blk, ndim_grid))
    if lay_out is None:
        out_spec = pl.BlockSpec((None,) + blk, lambda *ids: (ids[0],) + _state_tail(ids, blk, ndim_grid))
        out_shape = jax.ShapeDtypeStruct((nseq,) + _state_full(blk, ndim_grid), f32)
    else:
        out_spec = pl.BlockSpec((None, None) + blk, lambda *ids: (lay_out, ids[0]) + _state_tail(ids, blk, ndim_grid))
        out_shape = jax.ShapeDtypeStruct((DEPTH, nseq) + _state_full(blk, ndim_grid), f32)
    return in_spec, out_spec, out_shape, st["acc"]


def _state_tail(ids, blk, ndim_grid):
    return (0,) * len(blk) if ndim_grid == 2 else (ids[1],) + (0,) * (len(blk) - 1)


def _state_full(blk, ndim_grid):
    return blk if ndim_grid == 2 else (MB_HEADS,) + blk[1:]


def _wkv(p_rkv3, dec3, a3, g3, vs3, vf3, init_prev, params, st, *, row0, nseq, L, tc):
    assert L % tc == 0 and row0 % tc == 0
    nc = L // tc
    b0 = row0 // tc
    vres = vs3 is not None

    def pspec(col):
        return pl.BlockSpec((tc, WKV_PAIRS, LANES), lambda s, c: (b0 + s * nc + c, col // D_MODEL, 0))

    seq_spec = pl.BlockSpec((tc, WKV_PAIRS, LANES), lambda s, c: (b0 + s * nc + c, 0, 0))
    out_seq = pl.BlockSpec((tc, WKV_PAIRS, LANES), lambda s, c: (s * nc + c, 0, 0))
    st_in, st_out, st_shape, acc = _state_io(st, (WKV_PAIRS, RW_HEAD, LANES), nseq, 2)
    args = [p_rkv3, p_rkv3, p_rkv3, dec3, a3, g3]
    specs = [pspec(C_R), pspec(C_K), pspec(C_V), seq_spec, seq_spec, seq_spec]
    if vres:
        args += [vs3, vf3]
        specs += [seq_spec, seq_spec]
    args += [init_prev, params, st["s0"]]
    specs += [pl.BlockSpec((None, 3, WKV_PAIRS, LANES), lambda s, c: (s, 0, 0, 0)),
              pl.BlockSpec(params.shape, lambda s, c: (0, 0, 0)), st_in]
    n = nseq * L
    out_shape = [jax.ShapeDtypeStruct((n, WKV_PAIRS, LANES), bf16)]
    out_specs = [out_seq]
    if not vres:
        out_shape.append(jax.ShapeDtypeStruct((n, WKV_PAIRS, LANES), f32))
        out_specs.append(out_seq)
    out_shape.append(st_shape)
    out_specs.append(st_out)
    aliases = {}
    if acc is not None:
        args.append(acc)
        specs.append(pl.BlockSpec(memory_space=pl.ANY))
        aliases = {len(args) - 1: len(out_shape) - 1}
    chunk = pltpu.VMEM((tc, WKV_PAIRS, LANES), f32)
    return pl.pallas_call(
        functools.partial(_wkv_kernel, tc=tc, nc=nc, vres=vres, has_acc=acc is not None),
        grid=(nseq, nc), in_specs=specs, out_specs=out_specs, out_shape=out_shape,
        scratch_shapes=[pltpu.VMEM((WKV_PAIRS, RW_HEAD, LANES), f32), pltpu.VMEM((3, WKV_PAIRS, LANES), f32)] + [chunk] * 6,
        input_output_aliases=aliases,
        compiler_params=_cparams(("parallel", "arbitrary")),
        name="wkv7",
    )(*args)


def _wkv_state_to_kernel(s):
    lead = s.shape[:-3]
    n = len(lead)
    s = s.reshape(lead + (RW_HEADS, WKV_PAIRS, 2, WKV_PAIRS, 2))
    s = s.transpose(tuple(range(n)) + tuple(n + i for i in (3, 2, 1, 4, 0)))
    return s.reshape(lead + (WKV_PAIRS, RW_HEAD, LANES))


def _wkv_state_from_kernel(s):
    lead = s.shape[:-3]
    n = len(lead)
    s = s.reshape(lead + (WKV_PAIRS, 2, WKV_PAIRS, 2, RW_HEADS))
    s = s.transpose(tuple(range(n)) + tuple(n + i for i in (4, 2, 1, 0, 3)))
    return s.reshape(lead + (RW_HEADS, RW_HEAD, RW_HEAD))


CONV_HALO = SUBLANES


def _ssd_kernel(*refs, q, nc, has_acc):
    n_in = 18
    (x_ref, b_ref, c_ref, z_ref, dt_ref, a_ref, wx_ref, wb_ref, wc_ref, bx_ref, bb_ref, bc_ref,
     d_ref, nw_ref, ix_ref, ib_ref, ic_ref, h0_ref) = refs[:n_in]
    y_ref, hout_ref, h_ref, fx_ref, fb_ref, fc_ref = refs[n_in + (1 if has_acc else 0):]
    c = pl.program_id(2)

    @pl.when(c == 0)
    def _():
        h_ref[...] = h0_ref[...].reshape(h_ref.shape)
        fx_ref[0:CONV_HALO, :] = ix_ref[...]
        fb_ref[0:CONV_HALO, :] = ib_ref[...]
        fc_ref[0:CONV_HALO, :] = ic_ref[...]

    def conv_silu(raw_ref, f_ref, w_ref, bias_ref):
        f_ref[CONV_HALO:CONV_HALO + q, :] = raw_ref[...]
        w = w_ref[...]
        acc = bias_ref[...]
        for i in range(MB_CONV):
            o = CONV_HALO - (MB_CONV - 1) + i
            acc = acc + f_ref[o:o + q, :] * w[i:i + 1, :]
        f_ref[0:CONV_HALO, :] = f_ref[q:q + CONV_HALO, :]
        return acc * jax.nn.sigmoid(acc)

    x = conv_silu(x_ref, fx_ref, wx_ref, bx_ref)
    bm = conv_silu(b_ref, fb_ref, wb_ref, bb_ref)
    cm = conv_silu(c_ref, fc_ref, wc_ref, bc_ref)

    nt = (((1,), (1,)), ((), ()))
    tn = (((0,), (0,)), ((), ()))
    hi = lax.Precision.HIGHEST
    row = lax.broadcasted_iota(jnp.int32, (q, q), 0)
    col = lax.broadcasted_iota(jnp.int32, (q, q), 1)
    causal = row >= col
    tri = causal.astype(f32)

    hg = h_ref[...]
    cg = cm.astype(bf16)
    bg = bm.astype(bf16)
    cb = lax.dot_general(cg, bg, nt, preferred_element_type=f32)
    y_state = lax.dot_general(cg, hg.astype(bf16), nt, preferred_element_type=f32)
    dt = dt_ref[...]
    dta = dt * a_ref[...]
    acum = jnp.dot(tri, dta, preferred_element_type=f32, precision=hi)
    acum_t = lax.dot_general(dta, tri, (((0,), (1,)), ((), ())), preferred_element_type=f32, precision=hi)
    ys, xts = [], []
    for r in range(MB_HPG):
        a_col = acum[:, r:r + 1]
        seg = jnp.where(causal, a_col - acum_t[r:r + 1, :], -jnp.inf)
        wts = cb * jnp.exp(seg)
        xdt = x[:, r * MB_HEAD:(r + 1) * MB_HEAD] * dt[:, r:r + 1]
        y_r = jnp.dot(wts.astype(bf16), xdt.astype(bf16), preferred_element_type=f32)
        y_r = y_r + y_state[:, r * MB_HEAD:(r + 1) * MB_HEAD] * jnp.exp(a_col)
        ys.append(y_r)
        xts.append(xdt * jnp.exp(acum[q - 1:q, r:r + 1] - a_col))
    xt = jnp.concatenate(xts, axis=1).astype(bf16)
    upd = lax.dot_general(xt, bg, tn, preferred_element_type=f32)
    for r in range(MB_HPG):
        sl = slice(r * MB_HEAD, (r + 1) * MB_HEAD)
        h_ref[sl, :] = hg[sl, :] * jnp.exp(acum[q - 1:q, r:r + 1]) + upd[sl, :]

    y = jnp.concatenate(ys, axis=1) + x * d_ref[...]
    z = z_ref[...]
    gated = y * (z * jax.nn.sigmoid(z))
    gated = gated * lax.rsqrt(jnp.mean(gated * gated, axis=-1, keepdims=True) + GATED_NORM_EPS)
    y_ref[...] = (gated * nw_ref[...]).astype(y_ref.dtype)

    @pl.when(c == nc - 1)
    def _():
        hout_ref[...] = h_ref[...].reshape(hout_ref.shape)


def _ssd(p_mb, dtg, ag, lw, conv_init, conv_layer, st, *, row0, nseq, L, q):
    assert L % q == 0 and row0 % q == 0
    nc = L // q
    b0 = row0 // q
    gw = MB_HPG * MB_HEAD
    xg, bgc, cgc, zg = M_X // gw, M_B // MB_STATE, M_C // MB_STATE, M_Z // gw
    ib, ic = MB_INNER // MB_STATE, (MB_INNER + MB_BC) // MB_STATE

    def rows(width, col0):
        return pl.BlockSpec((q, width), lambda s, g, c: (b0 + s * nc + c, col0 + g))

    def per_group(nrows, width, col0=0):
        return pl.BlockSpec((nrows, width), lambda s, g, c: (0, col0 + g))

    def halo(width, col0=0):
        return pl.BlockSpec((None, None, CONV_HALO, width), lambda s, g, c: (conv_layer, s, 0, col0 + g))

    st_in, st_out, st_shape, acc = _state_io(st, (MB_HPG, MB_HEAD, MB_STATE), nseq, 3)
    in_specs = [rows(gw, xg), rows(MB_STATE, bgc), rows(MB_STATE, cgc), rows(gw, zg),
                pl.BlockSpec((None, q, MB_HPG), lambda s, g, c: (g, b0 + s * nc + c, 0)),
                pl.BlockSpec((None, 1, MB_HPG), lambda s, g, c: (g, 0, 0)),
                per_group(MB_CONV, gw), per_group(MB_CONV, MB_STATE, ib), per_group(MB_CONV, MB_STATE, ic),
                per_group(1, gw), per_group(1, MB_STATE, ib), per_group(1, MB_STATE, ic),
                per_group(1, gw), per_group(1, gw),
                halo(gw), halo(MB_STATE, ib), halo(MB_STATE, ic),
                st_in]
    cw, cbias = lw["conv_w"], lw["conv_b"]
    args = [p_mb, p_mb, p_mb, p_mb, dtg, ag, cw, cw, cw, cbias, cbias, cbias, lw["d"], lw["norm"],
            conv_init, conv_init, conv_init, st["s0"]]
    aliases = {}
    if acc is not None:
        args.append(acc)
        in_specs.append(pl.BlockSpec(memory_space=pl.ANY))
        aliases = {len(args) - 1: 1}
    return pl.pallas_call(
        functools.partial(_ssd_kernel, q=q, nc=nc, has_acc=acc is not None),
        grid=(nseq, MB_GROUPS, nc),
        in_specs=in_specs,
        out_specs=[pl.BlockSpec((q, gw), lambda s, g, c: (s * nc + c, g)), st_out],
        out_shape=[jax.ShapeDtypeStruct((nseq * L, MB_INNER), bf16), st_shape],
        scratch_shapes=[pltpu.VMEM((gw, MB_STATE), f32), pltpu.VMEM((q + CONV_HALO, gw), f32),
                        pltpu.VMEM((q + CONV_HALO, MB_STATE), f32), pltpu.VMEM((q + CONV_HALO, MB_STATE), f32)],
        input_output_aliases=aliases,
        compiler_params=_cparams(("parallel", "parallel", "arbitrary")),
        name="ssd",
    )(*args)


def _perm_last(w):
    s = w.shape[:-1]
    return w.reshape(s + (RW_HEADS, RW_HEAD)).swapaxes(-1, -2).reshape(s + (D_MODEL,))


def _perm_rows(w):
    return w.reshape(w.shape[:-2] + (RW_HEADS, RW_HEAD, w.shape[-1])).swapaxes(-3, -2).reshape(w.shape)


def _pad_last(w, n):
    return jnp.pad(w, [(0, 0)] * (w.ndim - 1) + [(0, n - w.shape[-1])])


def _perm_rkv(w):
    s = w.shape[:-1]
    return _perm_last(w.reshape(s + (3, D_MODEL))).reshape(s + (3 * D_MODEL,))


def _split_in_cols(w):
    rkv = _perm_rkv(w[..., :N_RKV])
    lora = w[..., N_RKV:RW_COLS]
    mb = w[..., RW_COLS:RW_COLS + N_MB]
    dt = w[..., RW_COLS + N_MB:RW_COLS + MB_COLS]
    gates = w[..., RW_COLS + MB_COLS:]
    small = jnp.concatenate([_pad_last(lora, S_LORA_PAD), _pad_last(dt, N_SMALL - S_LORA_PAD)], axis=-1)
    return rkv, gates, mb, small


def _rwkv_layer_weights(l, rw_mu, rw_w0, rw_w2, rw_a0, rw_a2, rw_g2, rw_kk, rw_ka, rw_rk, rw_lnx_w, rw_lnx_b,
                        rw_v0, rw_v1, rw_v2):
    def tile(v):
        return _perm_last(v).reshape(WKV_PAIRS, LANES)

    mu = rw_mu[l]
    params = jnp.stack([tile(mu[C_R:C_R + D_MODEL]), tile(mu[C_K:C_K + D_MODEL]), tile(mu[C_V:C_V + D_MODEL]),
                        tile(rw_kk[l]), tile(rw_ka[l]), rw_rk[l].T.reshape(WKV_PAIRS, LANES),
                        tile(rw_lnx_w[l]), tile(rw_lnx_b[l])])
    lw = dict(
        params=params,
        mu_s=_pad_last(mu[3 * D_MODEL:], N_SMALL).reshape(1, N_SMALL),
        w0=_perm_last(rw_w0[l]).reshape(1, D_MODEL), a0=_perm_last(rw_a0[l]).reshape(1, D_MODEL),
        w2=_perm_last(rw_w2[l]).astype(bf16), a2=_perm_last(rw_a2[l]).astype(bf16),
        g2=jnp.pad(_perm_last(rw_g2[l]), ((0, GATE_LORA_PAD - RW_GATE_LORA), (0, 0))).astype(bf16),
        mu_v=None, v0=None, v1=None, v2=None)
    if l > 0:
        lw.update(
            mu_v=_perm_last(mu[C_V:C_V + D_MODEL]).reshape(1, D_MODEL),
            v0=_perm_last(rw_v0[l - 1]).reshape(1, D_MODEL),
            v1=_pad_last(_perm_rows(rw_v1[l - 1]), MV_LORA_PAD).astype(bf16),
            v2=jnp.pad(_perm_last(rw_v2[l - 1]), ((0, MV_LORA_PAD - RW_MV_LORA), (0, 0))).astype(bf16))
    return lw


def _mamba_layer_weights(l, mb_conv_w, mb_conv_b, mb_dt_bias, mb_a_log, mb_d, mb_norm):
    return dict(conv_w=mb_conv_w[l], conv_b=mb_conv_b[l].reshape(1, MB_CONV_DIM), dt_bias=mb_dt_bias[l],
                a_log=mb_a_log[l], d=jnp.repeat(mb_d[l], MB_HEAD).reshape(1, MB_INNER),
                norm=mb_norm[l].reshape(1, MB_INNER))


def _prev_rows(t, groups, inits):
    out, off = [], 0
    for (nseq, L), init in zip(groups, inits):
        tt = t[off:off + nseq * L].reshape(nseq, L, t.shape[-1])
        out.append(jnp.concatenate([init[:, None, :], tt[:, :-1]], axis=1).reshape(nseq * L, t.shape[-1]))
        off += nseq * L
    return jnp.concatenate(out, axis=0)


def _tail_rows(t, groups, k):
    out, off = [], 0
    for nseq, L in groups:
        idx = (off + np.arange(nseq)[:, None] * L + (L - k) + np.arange(k)[None, :]).reshape(-1)
        out.append(jnp.take(t, jnp.asarray(idx, jnp.int32), axis=0).reshape(nseq, k, t.shape[-1]))
        off += nseq * L
    return out


def _rwkv_branch(p_rkv, p_small, shift_prev, wkv_sts, v_first, lw, groups):
    rows = p_rkv.shape[0]
    vres = lw["v1"] is not None
    sp_rkv = [_perm_rkv(s[:, :N_RKV]) for s in shift_prev]
    sp_small = [_pad_last(s[:, N_RKV:], N_SMALL) for s in shift_prev]
    prev_small = _prev_rows(p_small, groups, sp_small)
    prev_v = _prev_rows(p_rkv[:, C_V:], groups, [s[:, C_V:] for s in sp_rkv]) if vres else None
    lora = _rwkv_lora(p_small, prev_small, p_rkv, prev_v, lw)
    to3 = lambda t: t.reshape(rows, WKV_PAIRS, LANES)
    dec3, a3, g3 = to3(lora[0]), to3(lora[1]), to3(lora[2])
    vs3 = to3(lora[3]) if vres else None
    p_rkv3 = p_rkv.reshape(rows, N_RKV // LANES, LANES)
    ys, vfs, s_new, off = [], [], [], 0
    for (nseq, L), st, sp in zip(groups, wkv_sts, sp_rkv):
        outs = _wkv(p_rkv3, dec3, a3, g3, vs3, v_first, sp.reshape(nseq, 3, WKV_PAIRS, LANES), lw["params"], st,
                    row0=off, nseq=nseq, L=L, tc=min(L, 64))
        ys.append(outs[0])
        if not vres:
            vfs.append(outs[1])
        s_new.append(outs[-1])
        off += nseq * L
    y = jnp.concatenate(ys, axis=0).reshape(rows, D_MODEL)
    if not vres:
        v_first = jnp.concatenate(vfs, axis=0)
    shift_new = [jnp.concatenate([_perm_rkv(a[:, 0]), b[:, 0, :RW_LORA]], axis=-1)
                 for a, b in zip(_tail_rows(p_rkv, groups, 1), _tail_rows(p_small, groups, 1))]
    return y, shift_new, s_new, v_first


def _mamba_branch(p_mb, p_small, conv_sts, ssm_sts, lw, groups):
    rows = p_mb.shape[0]
    dt = jax.nn.softplus(p_small[:, S_DT:S_DT + MB_HEADS] + lw["dt_bias"])
    dtg = dt.reshape(rows, MB_GROUPS, MB_HPG).transpose(1, 0, 2)
    ag = (-jnp.exp(lw["a_log"])).reshape(MB_GROUPS, 1, MB_HPG)
    ys, ssm_new, off = [], [], 0
    for (nseq, L), (conv_init, conv_layer), st in zip(groups, conv_sts, ssm_sts):
        assert L >= MB_CONV - 1
        y_g, h_g = _ssd(p_mb, dtg, ag, lw, conv_init, conv_layer, st, row0=off, nseq=nseq, L=L, q=min(L, 128))
        ys.append(y_g)
        ssm_new.append(h_g)
        off += nseq * L
    conv_new = [t[:, :, M_X:] for t in _tail_rows(p_mb, groups, MB_CONV - 1)]
    return jnp.concatenate(ys, axis=0), conv_new, ssm_new


def kernel(x_prompt, x_sample, c_prompt, c_sample, state_rwkv_shift, state_rwkv_wkv, state_mamba_conv, state_mamba_ssm, ada_w, ada_b, norm1, norm2, w_in, rw_mu, rw_w0, rw_w2, rw_a0, rw_a2, rw_g2, rw_kk, rw_ka, rw_rk, rw_lnx_w, rw_lnx_b, rw_v0, rw_v1, rw_v2, mb_conv_w, mb_conv_b, mb_dt_bias, mb_a_log, mb_d, mb_norm, w_proj_a, w_proj_b, w_out, ffn_gate, ffn_up, ffn_down, norm_f):
    bp, lp, _ = x_prompt.shape
    bs, ls, _ = x_sample.shape
    groups = ((bp, lp), (bs, ls))
    rows = bp * lp + bs * ls
    assert lp % SUBLANES == 0 and ls == SUBLANES

    nc_rows = bp + bs
    c_pad = -(-nc_rows // 16) * 16
    c_all = jnp.pad(jnp.concatenate([c_prompt, c_sample], axis=0), ((0, c_pad - nc_rows), (0, 0)))
    mod = _mod_all(c_all, ada_w, ada_b)[:, :nc_rows].reshape(DEPTH, nc_rows, N_MOD, 1, D_MODEL)
    modg = jnp.concatenate([jnp.repeat(mod[:, :bp], lp // SUBLANES, axis=1), mod[:, bp:]], axis=1)

    x3 = jnp.concatenate([x_prompt.reshape(-1, D_MODEL), x_sample.reshape(-1, D_MODEL)], axis=0)
    x3 = x3.reshape(rows // SUBLANES, SUBLANES, D_MODEL)

    w_rkv, w_gate, w_mb, w_small = _split_in_cols(w_in.astype(bf16))
    w_pa, w_pb, w_o = _perm_rows(w_proj_a.astype(bf16)), w_proj_b.astype(bf16), w_out.astype(bf16)
    w_fg, w_fu = _pad_last(ffn_gate.astype(bf16), D_FF_PAD), _pad_last(ffn_up.astype(bf16), D_FF_PAD)
    w_fd = jnp.pad(ffn_down.astype(bf16), ((0, 0), (0, D_FF_PAD - D_FF), (0, 0)))

    zeros = functools.partial(jnp.zeros, dtype=f32)
    wkv_in = _wkv_state_to_kernel(state_rwkv_wkv)
    conv_in = jnp.pad(state_mamba_conv, ((0, 0), (0, 0), (CONV_HALO - (MB_CONV - 1), 0), (0, 0)))
    z_wkv = zeros((1, bp, WKV_PAIRS, RW_HEAD, LANES))
    z_conv = zeros((1, bp, CONV_HALO, MB_CONV_DIM))
    z_ssm = zeros((1, bp, MB_HEADS, MB_HEAD, MB_STATE))
    wkv_acc, ssm_acc = None, None

    delta3, v_first = None, None
    shifts, wkvs_p, convs, ssms_p = [], [], [], []
    for l in range(DEPTH):
        lw_rw = _rwkv_layer_weights(l, rw_mu, rw_w0, rw_w2, rw_a0, rw_a2, rw_g2, rw_kk, rw_ka, rw_rk, rw_lnx_w,
                                    rw_lnx_b, rw_v0, rw_v1, rw_v2)
        lw_mb = _mamba_layer_weights(l, mb_conv_w, mb_conv_b, mb_dt_bias, mb_a_log, mb_d, mb_norm)

        x3, h = _resnorm(x3, delta3, modg, (l - 1, 5), norm1[l], l, 1, 0)
        p_rkv = _mm(h, w_rkv, l, name="in_rkv")
        p_gate = _mm(h, w_gate, l, sigmoid=True, name="in_gate")
        p_mb = _mm(h, w_mb, l, name="in_mb")
        p_small = _mm(h, w_small, l, name="in_small")

        shift_prev = [zeros((bp, RW_COLS)), state_rwkv_shift[l]]
        wkv_sts = [dict(s0=z_wkv, layer=0, out_layer=None, acc=None),
                   dict(s0=wkv_in, layer=l, out_layer=l, acc=wkv_acc)]
        y_a, sh_new, wkv_new, v_first = _rwkv_branch(p_rkv, p_small, shift_prev, wkv_sts, v_first, lw_rw, groups)
        ssm_sts = [dict(s0=z_ssm, layer=0, out_layer=None, acc=None),
                   dict(s0=state_mamba_ssm, layer=l, out_layer=l, acc=ssm_acc)]
        y_b, conv_new, ssm_new = _mamba_branch(p_mb, p_small, [(z_conv, 0), (conv_in, l)], ssm_sts, lw_mb, groups)
        wkv_acc, ssm_acc = wkv_new[1], ssm_new[1]

        merged = _merge_mm(y_a, w_pa, y_b, w_pb, l, p_gate)
        mo = _mm(merged, w_o, l, name="w_out")
        x3, h2 = _resnorm(x3, mo.reshape(x3.shape), modg, (l, 2), norm2[l], l, 4, 3)
        hid = _swiglu_mm(h2, w_fg, w_fu, l)
        ff = _mm(hid, w_fd, l, tk=1024, name="ffn_down")
        delta3 = ff.reshape(x3.shape)

        shifts.append(sh_new)
        wkvs_p.append(wkv_new[0])
        convs.append(conv_new)
        ssms_p.append(ssm_new[0])

    y = _resnorm(x3, delta3, modg, (DEPTH - 1, 5), norm_f, None, None, None, final=True).reshape(rows, D_MODEL)
    y_prompt = y[:bp * lp].reshape(bp, lp, D_MODEL)
    y_sample = y[bp * lp:].reshape(bs, ls, D_MODEL)

    def stack(lst, gi):
        return jnp.stack([t[gi] for t in lst])

    return (y_prompt, y_sample,
            stack(shifts, 0), _wkv_state_from_kernel(jnp.stack(wkvs_p)), stack(convs, 0), jnp.stack(ssms_p),
            stack(shifts, 1), _wkv_state_from_kernel(wkv_acc), stack(convs, 1), ssm_acc)
```

```python
import functools

import jax
import jax.numpy as jnp
import numpy as np
from jax import lax
from jax.experimental import pallas as pl
from jax.experimental.pallas import tpu as pltpu

f32 = jnp.float32
bf16 = jnp.bfloat16

D_MODEL = 4096
DEPTH = 4
RW_HEAD = 64
RW_HEADS = D_MODEL // RW_HEAD
RW_DECAY_LORA = 128
RW_AAA_LORA = 128
RW_MV_LORA = 96
RW_GATE_LORA = 480
RW_LORA = RW_DECAY_LORA + RW_AAA_LORA + RW_GATE_LORA
RW_COLS = 3 * D_MODEL + RW_LORA
LNX_EPS = 64e-5
MB_INNER = 2 * D_MODEL
MB_HEAD = 64
MB_HEADS = MB_INNER // MB_HEAD
MB_GROUPS = 8
MB_HPG = MB_HEADS // MB_GROUPS
MB_STATE = 128
MB_CONV = 4
MB_BC = MB_GROUPS * MB_STATE
MB_CONV_DIM = MB_INNER + 2 * MB_BC
MB_COLS = MB_INNER + MB_CONV_DIM + MB_HEADS
GATE_COLS = 2 * D_MODEL
D_FF = 11008
N_MOD = 6
NORM_EPS = 1e-6
GATED_NORM_EPS = 1e-5

LANES = 128
SUBLANES = 8
GATE_LORA_PAD = 512
MV_LORA_PAD = 128
D_FF_PAD = 11264
VMEM_LIMIT = 52 * 1024 * 1024

C_R, C_K, C_V = 0, D_MODEL, 2 * D_MODEL
N_RKV = 3 * D_MODEL
M_Z = 0
M_X = MB_INNER
M_B = M_X + MB_INNER
M_C = M_B + MB_BC
N_MB = M_C + MB_BC
S_LORA_PAD = 768
S_DT = S_LORA_PAD
N_SMALL = 1024


def _cparams(sem):
    return pltpu.CompilerParams(dimension_semantics=sem, vmem_limit_bytes=VMEM_LIMIT)


def _mm_kernel(a_ref, b_ref, o_ref, acc_ref, *, nk, sigmoid):
    k = pl.program_id(2)

    @pl.when(k == 0)
    def _():
        acc_ref[...] = jnp.zeros_like(acc_ref)

    acc_ref[...] += jnp.dot(a_ref[...], b_ref[...], preferred_element_type=f32)

    @pl.when(k == nk - 1)
    def _():
        r = acc_ref[...]
        if sigmoid:
            r = jax.nn.sigmoid(r)
        o_ref[...] = r.astype(o_ref.dtype)


def _wspec(tk, tn, layer, index):
    return pl.BlockSpec((None, tk, tn), lambda i, j, k: (layer,) + index(i, j, k))


def _mm(a, b, layer, *, tm=1024, tn=1024, tk=2048, out_dtype=f32, sigmoid=False, name="mm"):
    m, kd = a.shape
    n = b.shape[-1]
    tm, tn, tk = min(tm, m), min(tn, n), min(tk, kd)
    assert m % tm == 0 and n % tn == 0 and kd % tk == 0, (a.shape, b.shape, tm, tn, tk)
    nk = kd // tk
    return pl.pallas_call(
        functools.partial(_mm_kernel, nk=nk, sigmoid=sigmoid),
        grid=(m // tm, n // tn, nk),
        in_specs=[pl.BlockSpec((tm, tk), lambda i, j, k: (i, k)),
                  _wspec(tk, tn, layer, lambda i, j, k: (k, j))],
        out_specs=pl.BlockSpec((tm, tn), lambda i, j, k: (i, j)),
        out_shape=jax.ShapeDtypeStruct((m, n), out_dtype),
        scratch_shapes=[pltpu.VMEM((tm, tn), f32)],
        compiler_params=_cparams(("parallel", "parallel", "arbitrary")),
        name=name,
    )(a, b)


def _swiglu_kernel(a_ref, bg_ref, bu_ref, o_ref, accg_ref, accu_ref, *, nk):
    k = pl.program_id(2)

    @pl.when(k == 0)
    def _():
        accg_ref[...] = jnp.zeros_like(accg_ref)
        accu_ref[...] = jnp.zeros_like(accu_ref)

    a = a_ref[...]
    accg_ref[...] += jnp.dot(a, bg_ref[...], preferred_element_type=f32)
    accu_ref[...] += jnp.dot(a, bu_ref[...], preferred_element_type=f32)

    @pl.when(k == nk - 1)
    def _():
        g = accg_ref[...]
        o_ref[...] = (g * jax.nn.sigmoid(g) * accu_ref[...]).astype(o_ref.dtype)


def _swiglu_mm(a, bg, bu, layer, *, tm=1024, tn=1024, tk=2048):
    m, kd = a.shape
    n = bg.shape[-1]
    tm = min(tm, m)
    assert m % tm == 0 and n % tn == 0 and kd % tk == 0
    nk = kd // tk
    return pl.pallas_call(
        functools.partial(_swiglu_kernel, nk=nk),
        grid=(m // tm, n // tn, nk),
        in_specs=[pl.BlockSpec((tm, tk), lambda i, j, k: (i, k)),
                  _wspec(tk, tn, layer, lambda i, j, k: (k, j)),
                  _wspec(tk, tn, layer, lambda i, j, k: (k, j))],
        out_specs=pl.BlockSpec((tm, tn), lambda i, j, k: (i, j)),
        out_shape=jax.ShapeDtypeStruct((m, n), bf16),
        scratch_shapes=[pltpu.VMEM((tm, tn), f32), pltpu.VMEM((tm, tn), f32)],
        compiler_params=_cparams(("parallel", "parallel", "arbitrary")),
        name="swiglu_mm",
    )(a, bg, bu)


def _merge_kernel(ya_ref, wa_ref, yb_ref, wb_ref, ga_ref, gb_ref, o_ref, acca_ref, accb_ref, *, nka, nkb):
    k = pl.program_id(2)

    @pl.when(k == 0)
    def _():
        acca_ref[...] = jnp.zeros_like(acca_ref)
        accb_ref[...] = jnp.zeros_like(accb_ref)

    @pl.when(k < nka)
    def _():
        acca_ref[...] += jnp.dot(ya_ref[...], wa_ref[...], preferred_element_type=f32)

    @pl.when(k >= nka)
    def _():
        accb_ref[...] += jnp.dot(yb_ref[...], wb_ref[...], preferred_element_type=f32)

    @pl.when(k == nka + nkb - 1)
    def _():
        o_ref[...] = (ga_ref[...] * acca_ref[...] + gb_ref[...] * accb_ref[...]).astype(o_ref.dtype)


def _merge_mm(ya, wa, yb, wb, layer, p_gate, *, tm=1024, tn=1024, tk=1024):
    m, ka = ya.shape
    kb = yb.shape[1]
    n = wa.shape[-1]
    tm = min(tm, m)
    assert m % tm == 0 and n % tn == 0 and ka % tk == 0 and kb % tk == 0
    nka, nkb = ka // tk, kb // tk
    ga0, gb0 = 0, n // tn
    return pl.pallas_call(
        functools.partial(_merge_kernel, nka=nka, nkb=nkb),
        grid=(m // tm, n // tn, nka + nkb),
        in_specs=[pl.BlockSpec((tm, tk), lambda i, j, k: (i, jnp.minimum(k, nka - 1))),
                  _wspec(tk, tn, layer, lambda i, j, k: (jnp.minimum(k, nka - 1), j)),
                  pl.BlockSpec((tm, tk), lambda i, j, k: (i, jnp.maximum(k - nka, 0))),
                  _wspec(tk, tn, layer, lambda i, j, k: (jnp.maximum(k - nka, 0), j)),
                  pl.BlockSpec((tm, tn), lambda i, j, k: (i, ga0 + j)),
                  pl.BlockSpec((tm, tn), lambda i, j, k: (i, gb0 + j))],
        out_specs=pl.BlockSpec((tm, tn), lambda i, j, k: (i, j)),
        out_shape=jax.ShapeDtypeStruct((m, n), bf16),
        scratch_shapes=[pltpu.VMEM((tm, tn), f32), pltpu.VMEM((tm, tn), f32)],
        compiler_params=_cparams(("parallel", "parallel", "arbitrary")),
        name="merge_mm",
    )(ya, wa, yb, wb, p_gate, p_gate)


def _mod_kernel(c_ref, w_ref, b_ref, o_ref):
    c = c_ref[...]
    a = (c * jax.nn.sigmoid(c)).astype(bf16)
    o_ref[...] = jnp.dot(a, w_ref[...].astype(bf16), preferred_element_type=f32) + b_ref[...]


def _mod_all(c_all, ada_w, ada_b, *, tn=512):
    rows = c_all.shape[0]
    n = N_MOD * D_MODEL
    return pl.pallas_call(
        _mod_kernel,
        grid=(DEPTH, n // tn),
        in_specs=[pl.BlockSpec((rows, D_MODEL), lambda l, j: (0, 0)),
                  pl.BlockSpec((None, D_MODEL, tn), lambda l, j: (l, 0, j)),
                  pl.BlockSpec((None, 1, tn), lambda l, j: (l, 0, j))],
        out_specs=pl.BlockSpec((None, rows, tn), lambda l, j: (l, 0, j)),
        out_shape=jax.ShapeDtypeStruct((DEPTH, rows, n), f32),
        compiler_params=_cparams(("parallel", "parallel")),
        name="adaln_mod",
    )(c_all, ada_w, ada_b.reshape(DEPTH, 1, n))


def _resnorm_kernel(*refs, has_delta, final):
    it = iter(refs)
    x_ref = next(it)
    if has_delta:
        d_ref, gate_ref = next(it), next(it)
    g_ref = next(it)
    if not final:
        sc_ref, sh_ref = next(it), next(it)
    x = x_ref[...]
    if has_delta:
        x = x + gate_ref[...] * d_ref[...]
    y = x * lax.rsqrt(jnp.mean(x * x, axis=-1, keepdims=True) + NORM_EPS) * g_ref[...]
    if final:
        o_ref = next(it)
        o_ref[...] = y
    else:
        xo_ref, h_ref = next(it), next(it)
        xo_ref[...] = x
        h = y * (1.0 + sc_ref[...]) + sh_ref[...]
        h_ref[...] = h.reshape(h_ref.shape).astype(bf16)


def _resnorm(x3, delta3, modg, gate_at, g, l, sc_idx, sh_idx, *, final=False, tb=16):
    ng = x3.shape[0]
    assert ng % tb == 0
    has_delta = delta3 is not None
    row_spec = pl.BlockSpec((tb, SUBLANES, D_MODEL), lambda i: (i, 0, 0))

    def mod_spec(layer, idx):
        return pl.BlockSpec((None, tb, None, 1, D_MODEL), lambda i: (layer, i, idx, 0, 0))

    args, specs = [x3], [row_spec]
    if has_delta:
        args += [delta3, modg]
        specs += [row_spec, mod_spec(*gate_at)]
    args.append(g.reshape(1, 1, D_MODEL))
    specs.append(pl.BlockSpec((1, 1, D_MODEL), lambda i: (0, 0, 0)))
    if final:
        out_shape = jax.ShapeDtypeStruct(x3.shape, f32)
        out_specs = row_spec
    else:
        args += [modg, modg]
        specs += [mod_spec(l, sc_idx), mod_spec(l, sh_idx)]
        out_shape = [jax.ShapeDtypeStruct(x3.shape, f32), jax.ShapeDtypeStruct((ng * SUBLANES, D_MODEL), bf16)]
        out_specs = [row_spec, pl.BlockSpec((tb * SUBLANES, D_MODEL), lambda i: (i, 0))]
    return pl.pallas_call(
        functools.partial(_resnorm_kernel, has_delta=has_delta, final=final),
        grid=(ng // tb,), in_specs=specs, out_specs=out_specs, out_shape=out_shape,
        compiler_params=_cparams(("parallel",)),
        name="resnorm_final" if final else "resnorm",
    )(*args)


def _softplus(x):
    return jnp.maximum(x, 0.0) + jnp.log1p(jnp.exp(-jnp.abs(x)))


def _lora_kernel(*refs, vres):
    it = iter(refs)
    ps_ref, pp_ref, mu_ref, w0_ref, a0_ref, w2_ref, a2_ref, g2_ref = (next(it) for _ in range(8))
    if vres:
        pv_ref, pvp_ref, muv_ref, v0_ref, v1_ref, v2_ref = (next(it) for _ in range(6))
    dec_ref, a_ref, g_ref = next(it), next(it), next(it)
    ps = ps_ref[...]
    q = ps + (pp_ref[...] - ps) * mu_ref[...]
    o_a, o_g = RW_DECAY_LORA, RW_DECAY_LORA + RW_AAA_LORA
    wd, ad, gd = q[:, :o_a], q[:, o_a:o_g], q[:, o_g:o_g + GATE_LORA_PAD]
    lw = w0_ref[...] + jnp.dot(jnp.tanh(wd).astype(bf16), w2_ref[...], preferred_element_type=f32)
    logw = -_softplus(-lw) - 0.5
    dec_ref[...] = jnp.exp(-jnp.exp(logw))
    a_ref[...] = jax.nn.sigmoid(a0_ref[...] + jnp.dot(ad.astype(bf16), a2_ref[...], preferred_element_type=f32))
    g_ref[...] = jnp.dot(jax.nn.sigmoid(gd).astype(bf16), g2_ref[...], preferred_element_type=f32)
    if vres:
        vs_ref = next(it)
        pv = pv_ref[...]
        qv = pv + (pvp_ref[...] - pv) * muv_ref[...]
        vlo = jnp.dot(qv.astype(bf16), v1_ref[...], preferred_element_type=f32)
        vs_ref[...] = jax.nn.sigmoid(v0_ref[...] + jnp.dot(vlo.astype(bf16), v2_ref[...], preferred_element_type=f32))


def _rwkv_lora(p_small, prev_small, p_rkv, prev_v, lw, *, tm=128):
    rows = p_small.shape[0]
    tm = min(tm, rows)
    assert rows % tm == 0
    vres = lw["v1"] is not None
    row_s = pl.BlockSpec((tm, N_SMALL), lambda i: (i, 0))
    row_d = pl.BlockSpec((tm, D_MODEL), lambda i: (i, 0))

    def full(a):
        return pl.BlockSpec(a.shape, lambda i: (0,) * a.ndim)

    consts = [lw["mu_s"], lw["w0"], lw["a0"], lw["w2"], lw["a2"], lw["g2"]]
    args = [p_small, prev_small] + consts
    specs = [row_s, row_s] + [full(a) for a in consts]
    n_out = 3
    if vres:
        consts_v = [lw["mu_v"], lw["v0"], lw["v1"], lw["v2"]]
        args += [p_rkv, prev_v] + consts_v
        specs += [pl.BlockSpec((tm, D_MODEL), lambda i: (i, C_V // D_MODEL)), row_d] + [full(a) for a in consts_v]
        n_out = 4
    return pl.pallas_call(
        functools.partial(_lora_kernel, vres=vres),
        grid=(rows // tm,), in_specs=specs,
        out_specs=[row_d] * n_out,
        out_shape=[jax.ShapeDtypeStruct((rows, D_MODEL), f32)] * n_out,
        compiler_params=_cparams(("parallel",)),
        name="rwkv_lora",
    )(*args)


WKV_PAIRS = RW_HEAD // 2
(P_MU_R, P_MU_K, P_MU_V, P_KK, P_KA, P_RK, P_LNW, P_LNB) = range(8)


def _wkv_kernel(*refs, tc, nc, vres, n_alias):
    it = iter(refs)
    pr_ref, pk_ref, pv_ref, w_ref, a_ref, g_ref = (next(it) for _ in range(6))
    if vres:
        vs_ref, vf_ref = next(it), next(it)
    init_ref, par_ref, s0_ref = next(it), next(it), next(it)
    for _ in range(n_alias):
        next(it)
    y_ref = next(it)
    if not vres:
        vfo_ref = next(it)
    sout_ref = next(it)
    s_ref, carry_ref, r_s, k_s, v_s, a_s, b_s, y_s = (next(it) for _ in range(8))
    c = pl.program_id(1)

    @pl.when(c == 0)
    def _():
        s_ref[...] = s0_ref[...]
        carry_ref[...] = init_ref[...]

    def both(x):
        return x + pltpu.roll(x, RW_HEADS, axis=x.ndim - 1)

    def head_sum(x):
        return both(jnp.sum(x, axis=1, keepdims=True))

    def shifted(idx, ref):
        x = ref[...]
        prev = jnp.concatenate([carry_ref[idx][None], x[:-1]], axis=0)
        carry_ref[idx] = x[tc - 1]
        return x + (prev - x) * par_ref[idx]

    r = shifted(P_MU_R, pr_ref)
    k = shifted(P_MU_K, pk_ref)
    v = shifted(P_MU_V, pv_ref)
    if vres:
        v = v + (vf_ref[...] - v) * vs_ref[...]
    else:
        vfo_ref[...] = v
    a = a_ref[...]
    kk = k * par_ref[P_KK]
    kk = kk / jnp.maximum(jnp.sqrt(head_sum(kk * kk)), 1e-12)
    k = k * (1.0 + (a - 1.0) * par_ref[P_KA])
    r_s[...] = r
    k_s[...] = k
    v_s[...] = v
    a_s[...] = -kk
    b_s[...] = kk * a

    lane = lax.broadcasted_iota(jnp.int32, (WKV_PAIRS, LANES), 1)
    low = lane < RW_HEADS

    def step(t, carry):
        r_t, w_t, k_t, v_t, a_t, b_t = r_s[t], w_ref[t], k_s[t], v_s[t], a_s[t], b_s[t]
        wr = w_t * r_t
        sa = jnp.zeros((RW_HEAD, LANES), f32)
        yo = jnp.zeros((RW_HEAD, LANES), f32)
        for p in range(WKV_PAIRS):
            sp = s_ref[p]
            sa = sa + sp * a_t[p:p + 1, :]
            yo = yo + sp * wr[p:p + 1, :]
        sa = both(sa)
        yo = both(yo)
        br = both(jnp.sum(b_t * r_t, axis=0, keepdims=True))
        kr = both(jnp.sum(k_t * r_t, axis=0, keepdims=True))
        v_sw = pltpu.roll(v_t, RW_HEADS, axis=1)
        v_full = jnp.concatenate([jnp.where(low, v_t, v_sw), jnp.where(low, v_sw, v_t)], axis=0)
        for p in range(WKV_PAIRS):
            s_ref[p] = s_ref[p] * w_t[p:p + 1, :] + sa * b_t[p:p + 1, :] + v_full * k_t[p:p + 1, :]
        y = yo + sa * br + v_full * kr
        y_s[t] = jnp.where(low, y[:WKV_PAIRS], y[WKV_PAIRS:])
        return carry

    lax.fori_loop(0, tc, step, 0)

    y = y_s[...]
    mean = head_sum(y) * (1.0 / RW_HEAD)
    d = y - mean
    var = head_sum(d * d) * (1.0 / RW_HEAD)
    yn = d * lax.rsqrt(var + LNX_EPS) * par_ref[P_LNW] + par_ref[P_LNB]
    bonus = head_sum(r * k * par_ref[P_RK]) * v
    y_ref[...] = ((yn + bonus) * g_ref[...]).astype(y_ref.dtype)

    @pl.when(c == nc - 1)
    def _():
        sout_ref[...] = s_ref[...]


def _state_io(st, blk, nseq, ndim_grid):
    lay_in, lay_out = st["layer"], st["out_layer"]
    in_spec = pl.BlockSpec((None, None) + blk, lambda *ids: (lay_in, ids[0]) + _state_tail(ids, blk, ndim_grid))
    if lay_out is None:
        out_spec = pl.BlockSpec((None,) + blk, lambda *ids: (ids[0],) + _state_tail(ids, blk, ndim_grid))
        out_shape = jax.ShapeDtypeStruct((nseq,) + _state_full(blk, ndim_grid), f32)
    else:
        out_spec = pl.BlockSpec((None, None) + blk, lambda *ids: (lay_out, ids[0]) + _state_tail(ids, blk, ndim_grid))
        out_shape = jax.ShapeDtypeStruct((DEPTH, nseq) + _state_full(blk, ndim_grid), f32)
    return in_spec, out_spec, out_shape, st["acc"]


def _state_tail(ids, blk, ndim_grid):
    return (0,) * len(blk) if ndim_grid == 2 else (ids[1],) + (0,) * (len(blk) - 1)


def _state_full(blk, ndim_grid):
    return blk if ndim_grid == 2 else (MB_HEADS,) + blk[1:]


def _wkv(p_rkv3, dec3, a3, g3, vs3, vf3, init_prev, params, st, row_acc, *, row0, nseq, L, tc):
    assert L % tc == 0 and row0 % tc == 0
    nc = L // tc
    b0 = row0 // tc
    vres = vs3 is not None

    def pspec(col):
        return pl.BlockSpec((tc, WKV_PAIRS, LANES), lambda s, c: (b0 + s * nc + c, col // D_MODEL, 0))

    seq_spec = pl.BlockSpec((tc, WKV_PAIRS, LANES), lambda s, c: (b0 + s * nc + c, 0, 0))
    out_seq = seq_spec
    st_in, st_out, st_shape, acc = _state_io(st, (WKV_PAIRS, RW_HEAD, LANES), nseq, 2)
    args = [p_rkv3, p_rkv3, p_rkv3, dec3, a3, g3]
    specs = [pspec(C_R), pspec(C_K), pspec(C_V), seq_spec, seq_spec, seq_spec]
    if vres:
        args += [vs3, vf3]
        specs += [seq_spec, seq_spec]
    args += [init_prev, params, st["s0"]]
    specs += [pl.BlockSpec((None, 3, WKV_PAIRS, LANES), lambda s, c: (s, 0, 0, 0)),
              pl.BlockSpec(params.shape, lambda s, c: (0, 0, 0)), st_in]
    n = dec3.shape[0]
    out_shape = [jax.ShapeDtypeStruct((n, WKV_PAIRS, LANES), bf16)]
    out_specs = [out_seq]
    if not vres:
        out_shape.append(jax.ShapeDtypeStruct((n, WKV_PAIRS, LANES), f32))
        out_specs.append(out_seq)
    out_shape.append(st_shape)
    out_specs.append(st_out)
    aliases = {}
    alias_in = ([] if row_acc is None else list(row_acc)) + ([] if acc is None else [acc])
    alias_out = ([] if row_acc is None else list(range(len(row_acc)))) + ([] if acc is None else [len(out_shape) - 1])
    for arr, out_idx in zip(alias_in, alias_out):
        args.append(arr)
        specs.append(pl.BlockSpec(memory_space=pl.ANY))
        aliases[len(args) - 1] = out_idx
    chunk = pltpu.VMEM((tc, WKV_PAIRS, LANES), f32)
    return pl.pallas_call(
        functools.partial(_wkv_kernel, tc=tc, nc=nc, vres=vres, n_alias=len(alias_in)),
        grid=(nseq, nc), in_specs=specs, out_specs=out_specs, out_shape=out_shape,
        scratch_shapes=[pltpu.VMEM((WKV_PAIRS, RW_HEAD, LANES), f32), pltpu.VMEM((3, WKV_PAIRS, LANES), f32)] + [chunk] * 6,
        input_output_aliases=aliases,
        compiler_params=_cparams(("parallel", "arbitrary")),
        name="wkv7",
    )(*args)


def _wkv_state_to_kernel(s):
    lead = s.shape[:-3]
    n = len(lead)
    s = s.reshape(lead + (RW_HEADS, WKV_PAIRS, 2, WKV_PAIRS, 2))
    s = s.transpose(tuple(range(n)) + tuple(n + i for i in (3, 2, 1, 4, 0)))
    return s.reshape(lead + (WKV_PAIRS, RW_HEAD, LANES))


def _wkv_state_from_kernel(s):
    lead = s.shape[:-3]
    n = len(lead)
    s = s.reshape(lead + (WKV_PAIRS, 2, WKV_PAIRS, 2, RW_HEADS))
    s = s.transpose(tuple(range(n)) + tuple(n + i for i in (4, 2, 1, 0, 3)))
    return s.reshape(lead + (RW_HEADS, RW_HEAD, RW_HEAD))


CONV_HALO = SUBLANES


def _ssd_kernel(*refs, q, nc, n_alias):
    n_in = 18
    (x_ref, b_ref, c_ref, z_ref, dt_ref, a_ref, wx_ref, wb_ref, wc_ref, bx_ref, bb_ref, bc_ref,
     d_ref, nw_ref, ix_ref, ib_ref, ic_ref, h0_ref) = refs[:n_in]
    y_ref, hout_ref, h_ref, fx_ref, fb_ref, fc_ref = refs[n_in + n_alias:]
    c = pl.program_id(2)

    @pl.when(c == 0)
    def _():
        h_ref[...] = h0_ref[...].reshape(h_ref.shape)
        fx_ref[0:CONV_HALO, :] = ix_ref[...]
        fb_ref[0:CONV_HALO, :] = ib_ref[...]
        fc_ref[0:CONV_HALO, :] = ic_ref[...]

    def conv_silu(raw_ref, f_ref, w_ref, bias_ref):
        f_ref[CONV_HALO:CONV_HALO + q, :] = raw_ref[...]
        w = w_ref[...]
        acc = bias_ref[...]
        for i in range(MB_CONV):
            o = CONV_HALO - (MB_CONV - 1) + i
            acc = acc + f_ref[o:o + q, :] * w[i:i + 1, :]
        f_ref[0:CONV_HALO, :] = f_ref[q:q + CONV_HALO, :]
        return acc * jax.nn.sigmoid(acc)

    x = conv_silu(x_ref, fx_ref, wx_ref, bx_ref)
    bm = conv_silu(b_ref, fb_ref, wb_ref, bb_ref)
    cm = conv_silu(c_ref, fc_ref, wc_ref, bc_ref)

    nt = (((1,), (1,)), ((), ()))
    tn = (((0,), (0,)), ((), ()))
    hi = lax.Precision.HIGHEST
    row = lax.broadcasted_iota(jnp.int32, (q, q), 0)
    col = lax.broadcasted_iota(jnp.int32, (q, q), 1)
    causal = row >= col
    tri = causal.astype(f32)

    hg = h_ref[...]
    cg = cm.astype(bf16)
    bg = bm.astype(bf16)
    cb = lax.dot_general(cg, bg, nt, preferred_element_type=f32)
    y_state = lax.dot_general(cg, hg.astype(bf16), nt, preferred_element_type=f32)
    dt = dt_ref[...]
    dta = dt * a_ref[...]
    acum = jnp.dot(tri, dta, preferred_element_type=f32, precision=hi)
    acum_t = lax.dot_general(dta, tri, (((0,), (1,)), ((), ())), preferred_element_type=f32, precision=hi)
    ys, xts = [], []
    for r in range(MB_HPG):
        a_col = acum[:, r:r + 1]
        seg = jnp.where(causal, a_col - acum_t[r:r + 1, :], -jnp.inf)
        wts = cb * jnp.exp(seg)
        xdt = x[:, r * MB_HEAD:(r + 1) * MB_HEAD] * dt[:, r:r + 1]
        y_r = jnp.dot(wts.astype(bf16), xdt.astype(bf16), preferred_element_type=f32)
        y_r = y_r + y_state[:, r * MB_HEAD:(r + 1) * MB_HEAD] * jnp.exp(a_col)
        ys.append(y_r)
        xts.append(xdt * jnp.exp(acum[q - 1:q, r:r + 1] - a_col))
    xt = jnp.concatenate(xts, axis=1).astype(bf16)
    upd = lax.dot_general(xt, bg, tn, preferred_element_type=f32)
    for r in range(MB_HPG):
        sl = slice(r * MB_HEAD, (r + 1) * MB_HEAD)
        h_ref[sl, :] = hg[sl, :] * jnp.exp(acum[q - 1:q, r:r + 1]) + upd[sl, :]

    y = jnp.concatenate(ys, axis=1) + x * d_ref[...]
    z = z_ref[...]
    gated = y * (z * jax.nn.sigmoid(z))
    gated = gated * lax.rsqrt(jnp.mean(gated * gated, axis=-1, keepdims=True) + GATED_NORM_EPS)
    y_ref[...] = (gated * nw_ref[...]).astype(y_ref.dtype)

    @pl.when(c == nc - 1)
    def _():
        hout_ref[...] = h_ref[...].reshape(hout_ref.shape)


def _ssd(p_mb, dtg, ag, lw, conv_init, conv_layer, st, y_acc, *, row0, nseq, L, q):
    assert L % q == 0 and row0 % q == 0
    nc = L // q
    b0 = row0 // q
    gw = MB_HPG * MB_HEAD
    xg, bgc, cgc, zg = M_X // gw, M_B // MB_STATE, M_C // MB_STATE, M_Z // gw
    ib, ic = MB_INNER // MB_STATE, (MB_INNER + MB_BC) // MB_STATE

    def rows(width, col0):
        return pl.BlockSpec((q, width), lambda s, g, c: (b0 + s * nc + c, col0 + g))

    def per_group(nrows, width, col0=0):
        return pl.BlockSpec((nrows, width), lambda s, g, c: (0, col0 + g))

    def halo(width, col0=0):
        return pl.BlockSpec((None, None, CONV_HALO, width), lambda s, g, c: (conv_layer, s, 0, col0 + g))

    st_in, st_out, st_shape, acc = _state_io(st, (MB_HPG, MB_HEAD, MB_STATE), nseq, 3)
    in_specs = [rows(gw, xg), rows(MB_STATE, bgc), rows(MB_STATE, cgc), rows(gw, zg),
                pl.BlockSpec((None, q, MB_HPG), lambda s, g, c: (g, b0 + s * nc + c, 0)),
                pl.BlockSpec((None, 1, MB_HPG), lambda s, g, c: (g, 0, 0)),
                per_group(MB_CONV, gw), per_group(MB_CONV, MB_STATE, ib), per_group(MB_CONV, MB_STATE, ic),
                per_group(1, gw), per_group(1, MB_STATE, ib), per_group(1, MB_STATE, ic),
                per_group(1, gw), per_group(1, gw),
                halo(gw), halo(MB_STATE, ib), halo(MB_STATE, ic),
                st_in]
    cw, cbias = lw["conv_w"], lw["conv_b"]
    args = [p_mb, p_mb, p_mb, p_mb, dtg, ag, cw, cw, cw, cbias, cbias, cbias, lw["d"], lw["norm"],
            conv_init, conv_init, conv_init, st["s0"]]
    aliases = {}
    for arr, out_idx in ((y_acc, 0), (acc, 1)):
        if arr is not None:
            args.append(arr)
            in_specs.append(pl.BlockSpec(memory_space=pl.ANY))
            aliases[len(args) - 1] = out_idx
    return pl.pallas_call(
        functools.partial(_ssd_kernel, q=q, nc=nc, n_alias=len(aliases)),
        grid=(nseq, MB_GROUPS, nc),
        in_specs=in_specs,
        out_specs=[pl.BlockSpec((q, gw), lambda s, g, c: (b0 + s * nc + c, g)), st_out],
        out_shape=[jax.ShapeDtypeStruct((p_mb.shape[0], MB_INNER), bf16), st_shape],
        scratch_shapes=[pltpu.VMEM((gw, MB_STATE), f32), pltpu.VMEM((q + CONV_HALO, gw), f32),
                        pltpu.VMEM((q + CONV_HALO, MB_STATE), f32), pltpu.VMEM((q + CONV_HALO, MB_STATE), f32)],
        input_output_aliases=aliases,
        compiler_params=_cparams(("parallel", "parallel", "arbitrary")),
        name="ssd",
    )(*args)


def _perm_last(w):
    s = w.shape[:-1]
    return w.reshape(s + (RW_HEADS, RW_HEAD)).swapaxes(-1, -2).reshape(s + (D_MODEL,))


def _perm_rows(w):
    return w.reshape(w.shape[:-2] + (RW_HEADS, RW_HEAD, w.shape[-1])).swapaxes(-3, -2).reshape(w.shape)


def _pad_last(w, n):
    return jnp.pad(w, [(0, 0)] * (w.ndim - 1) + [(0, n - w.shape[-1])])


def _perm_rkv(w):
    s = w.shape[:-1]
    return _perm_last(w.reshape(s + (3, D_MODEL))).reshape(s + (3 * D_MODEL,))


def _split_in_cols(w):
    rkv = _perm_rkv(w[..., :N_RKV])
    lora = w[..., N_RKV:RW_COLS]
    mb = w[..., RW_COLS:RW_COLS + N_MB]
    dt = w[..., RW_COLS + N_MB:RW_COLS + MB_COLS]
    gates = w[..., RW_COLS + MB_COLS:]
    small = jnp.concatenate([_pad_last(lora, S_LORA_PAD), _pad_last(dt, N_SMALL - S_LORA_PAD)], axis=-1)
    return rkv, gates, mb, small


def _rwkv_layer_weights(l, rw_mu, rw_w0, rw_w2, rw_a0, rw_a2, rw_g2, rw_kk, rw_ka, rw_rk, rw_lnx_w, rw_lnx_b,
                        rw_v0, rw_v1, rw_v2):
    def tile(v):
        return _perm_last(v).reshape(WKV_PAIRS, LANES)

    mu = rw_mu[l]
    params = jnp.stack([tile(mu[C_R:C_R + D_MODEL]), tile(mu[C_K:C_K + D_MODEL]), tile(mu[C_V:C_V + D_MODEL]),
                        tile(rw_kk[l]), tile(rw_ka[l]), rw_rk[l].T.reshape(WKV_PAIRS, LANES),
                        tile(rw_lnx_w[l]), tile(rw_lnx_b[l])])
    lw = dict(
        params=params,
        mu_s=_pad_last(mu[3 * D_MODEL:], N_SMALL).reshape(1, N_SMALL),
        w0=_perm_last(rw_w0[l]).reshape(1, D_MODEL), a0=_perm_last(rw_a0[l]).reshape(1, D_MODEL),
        w2=_perm_last(rw_w2[l]).astype(bf16), a2=_perm_last(rw_a2[l]).astype(bf16),
        g2=jnp.pad(_perm_last(rw_g2[l]), ((0, GATE_LORA_PAD - RW_GATE_LORA), (0, 0))).astype(bf16),
        mu_v=None, v0=None, v1=None, v2=None)
    if l > 0:
        lw.update(
            mu_v=_perm_last(mu[C_V:C_V + D_MODEL]).reshape(1, D_MODEL),
            v0=_perm_last(rw_v0[l - 1]).reshape(1, D_MODEL),
            v1=_pad_last(_perm_rows(rw_v1[l - 1]), MV_LORA_PAD).astype(bf16),
            v2=jnp.pad(_perm_last(rw_v2[l - 1]), ((0, MV_LORA_PAD - RW_MV_LORA), (0, 0))).astype(bf16))
    return lw


def _mamba_layer_weights(l, mb_conv_w, mb_conv_b, mb_dt_bias, mb_a_log, mb_d, mb_norm):
    return dict(conv_w=mb_conv_w[l], conv_b=mb_conv_b[l].reshape(1, MB_CONV_DIM), dt_bias=mb_dt_bias[l],
                a_log=mb_a_log[l], d=jnp.repeat(mb_d[l], MB_HEAD).reshape(1, MB_INNER),
                norm=mb_norm[l].reshape(1, MB_INNER))


def _prev_rows(t, groups, inits):
    out, off = [], 0
    for (nseq, L), init in zip(groups, inits):
        tt = t[off:off + nseq * L].reshape(nseq, L, t.shape[-1])
        out.append(jnp.concatenate([init[:, None, :], tt[:, :-1]], axis=1).reshape(nseq * L, t.shape[-1]))
        off += nseq * L
    return jnp.concatenate(out, axis=0)


def _tail_rows(t, groups, k):
    out, off = [], 0
    for nseq, L in groups:
        idx = (off + np.arange(nseq)[:, None] * L + (L - k) + np.arange(k)[None, :]).reshape(-1)
        out.append(jnp.take(t, jnp.asarray(idx, jnp.int32), axis=0).reshape(nseq, k, t.shape[-1]))
        off += nseq * L
    return out


def _rwkv_branch(p_rkv, p_small, shift_prev, wkv_sts, v_first, lw, groups):
    rows = p_rkv.shape[0]
    vres = lw["v1"] is not None
    sp_rkv = [_perm_rkv(s[:, :N_RKV]) for s in shift_prev]
    sp_small = [_pad_last(s[:, N_RKV:], N_SMALL) for s in shift_prev]
    prev_small = _prev_rows(p_small, groups, sp_small)
    prev_v = _prev_rows(p_rkv[:, C_V:], groups, [s[:, C_V:] for s in sp_rkv]) if vres else None
    lora = _rwkv_lora(p_small, prev_small, p_rkv, prev_v, lw)
    to3 = lambda t: t.reshape(rows, WKV_PAIRS, LANES)
    dec3, a3, g3 = to3(lora[0]), to3(lora[1]), to3(lora[2])
    vs3 = to3(lora[3]) if vres else None
    p_rkv3 = p_rkv.reshape(rows, N_RKV // LANES, LANES)
    row_acc, s_new, off = None, [], 0
    for (nseq, L), st, sp in zip(groups, wkv_sts, sp_rkv):
        outs = _wkv(p_rkv3, dec3, a3, g3, vs3, v_first, sp.reshape(nseq, 3, WKV_PAIRS, LANES), lw["params"], st,
                    row_acc, row0=off, nseq=nseq, L=L, tc=min(L, 64))
        row_acc = outs[:-1]
        s_new.append(outs[-1])
        off += nseq * L
    y = row_acc[0].reshape(rows, D_MODEL)
    if not vres:
        v_first = row_acc[1]
    shift_new = [jnp.concatenate([_perm_rkv(a[:, 0]), b[:, 0, :RW_LORA]], axis=-1)
                 for a, b in zip(_tail_rows(p_rkv, groups, 1), _tail_rows(p_small, groups, 1))]
    return y, shift_new, s_new, v_first


def _mamba_branch(p_mb, p_small, conv_sts, ssm_sts, lw, groups):
    rows = p_mb.shape[0]
    dt = jax.nn.softplus(p_small[:, S_DT:S_DT + MB_HEADS] + lw["dt_bias"])
    dtg = dt.reshape(rows, MB_GROUPS, MB_HPG).transpose(1, 0, 2)
    ag = (-jnp.exp(lw["a_log"])).reshape(MB_GROUPS, 1, MB_HPG)
    y, ssm_new, off = None, [], 0
    for (nseq, L), (conv_init, conv_layer), st in zip(groups, conv_sts, ssm_sts):
        assert L >= MB_CONV - 1
        y, h_g = _ssd(p_mb, dtg, ag, lw, conv_init, conv_layer, st, y, row0=off, nseq=nseq, L=L, q=min(L, 128))
        ssm_new.append(h_g)
        off += nseq * L
    conv_new = [t[:, :, M_X:] for t in _tail_rows(p_mb, groups, MB_CONV - 1)]
    return y, conv_new, ssm_new


def kernel(x_prompt, x_sample, c_prompt, c_sample, state_rwkv_shift, state_rwkv_wkv, state_mamba_conv, state_mamba_ssm, ada_w, ada_b, norm1, norm2, w_in, rw_mu, rw_w0, rw_w2, rw_a0, rw_a2, rw_g2, rw_kk, rw_ka, rw_rk, rw_lnx_w, rw_lnx_b, rw_v0, rw_v1, rw_v2, mb_conv_w, mb_conv_b, mb_dt_bias, mb_a_log, mb_d, mb_norm, w_proj_a, w_proj_b, w_out, ffn_gate, ffn_up, ffn_down, norm_f):
    bp, lp, _ = x_prompt.shape
    bs, ls, _ = x_sample.shape
    groups = ((bp, lp), (bs, ls))
    rows = bp * lp + bs * ls
    assert lp % SUBLANES == 0 and ls == SUBLANES

    nc_rows = bp + bs
    c_pad = -(-nc_rows // 16) * 16
    c_all = jnp.pad(jnp.concatenate([c_prompt, c_sample], axis=0), ((0, c_pad - nc_rows), (0, 0)))
    mod = _mod_all(c_all, ada_w, ada_b)[:, :nc_rows].reshape(DEPTH, nc_rows, N_MOD, 1, D_MODEL)
    modg = jnp.concatenate([jnp.repeat(mod[:, :bp], lp // SUBLANES, axis=1), mod[:, bp:]], axis=1)

    x3 = jnp.concatenate([x_prompt.reshape(-1, D_MODEL), x_sample.reshape(-1, D_MODEL)], axis=0)
    x3 = x3.reshape(rows // SUBLANES, SUBLANES, D_MODEL)

    w_rkv, w_gate, w_mb, w_small = _split_in_cols(w_in.astype(bf16))
    w_pa, w_pb, w_o = _perm_rows(w_proj_a.astype(bf16)), w_proj_b.astype(bf16), w_out.astype(bf16)
    w_fg, w_fu = _pad_last(ffn_gate.astype(bf16), D_FF_PAD), _pad_last(ffn_up.astype(bf16), D_FF_PAD)
    w_fd = jnp.pad(ffn_down.astype(bf16), ((0, 0), (0, D_FF_PAD - D_FF), (0, 0)))

    zeros = functools.partial(jnp.zeros, dtype=f32)
    wkv_in = _wkv_state_to_kernel(state_rwkv_wkv)
    conv_in = jnp.pad(state_mamba_conv, ((0, 0), (0, 0), (CONV_HALO - (MB_CONV - 1), 0), (0, 0)))
    z_wkv = zeros((1, bp, WKV_PAIRS, RW_HEAD, LANES))
    z_conv = zeros((1, bp, CONV_HALO, MB_CONV_DIM))
    z_ssm = zeros((1, bp, MB_HEADS, MB_HEAD, MB_STATE))
    wkv_acc, ssm_acc = None, None

    delta3, v_first = None, None
    shifts, wkvs_p, convs, ssms_p = [], [], [], []
    for l in range(DEPTH):
        lw_rw = _rwkv_layer_weights(l, rw_mu, rw_w0, rw_w2, rw_a0, rw_a2, rw_g2, rw_kk, rw_ka, rw_rk, rw_lnx_w,
                                    rw_lnx_b, rw_v0, rw_v1, rw_v2)
        lw_mb = _mamba_layer_weights(l, mb_conv_w, mb_conv_b, mb_dt_bias, mb_a_log, mb_d, mb_norm)

        x3, h = _resnorm(x3, delta3, modg, (l - 1, 5), norm1[l], l, 1, 0)
        p_rkv = _mm(h, w_rkv, l, name="in_rkv")
        p_gate = _mm(h, w_gate, l, sigmoid=True, name="in_gate")
        p_mb = _mm(h, w_mb, l, name="in_mb")
        p_small = _mm(h, w_small, l, name="in_small")

        shift_prev = [zeros((bp, RW_COLS)), state_rwkv_shift[l]]
        wkv_sts = [dict(s0=z_wkv, layer=0, out_layer=None, acc=None),
                   dict(s0=wkv_in, layer=l, out_layer=l, acc=wkv_acc)]
        y_a, sh_new, wkv_new, v_first = _rwkv_branch(p_rkv, p_small, shift_prev, wkv_sts, v_first, lw_rw, groups)
        ssm_sts = [dict(s0=z_ssm, layer=0, out_layer=None, acc=None),
                   dict(s0=state_mamba_ssm, layer=l, out_layer=l, acc=ssm_acc)]
        y_b, conv_new, ssm_new = _mamba_branch(p_mb, p_small, [(z_conv, 0), (conv_in, l)], ssm_sts, lw_mb, groups)
        wkv_acc, ssm_acc = wkv_new[1], ssm_new[1]

        merged = _merge_mm(y_a, w_pa, y_b, w_pb, l, p_gate)
        mo = _mm(merged, w_o, l, name="w_out")
        x3, h2 = _resnorm(x3, mo.reshape(x3.shape), modg, (l, 2), norm2[l], l, 4, 3)
        hid = _swiglu_mm(h2, w_fg, w_fu, l)
        ff = _mm(hid, w_fd, l, tk=1024, name="ffn_down")
        delta3 = ff.reshape(x3.shape)

        shifts.append(sh_new)
        wkvs_p.append(wkv_new[0])
        convs.append(conv_new)
        ssms_p.append(ssm_new[0])

    y = _resnorm(x3, delta3, modg, (DEPTH - 1, 5), norm_f, None, None, None, final=True).reshape(rows, D_MODEL)
    y_prompt = y[:bp * lp].reshape(bp, lp, D_MODEL)
    y_sample = y[bp * lp:].reshape(bs, ls, D_MODEL)

    def stack(lst, gi):
        return jnp.stack([t[gi] for t in lst])

    return (y_prompt, y_sample,
            stack(shifts, 0), _wkv_state_from_kernel(jnp.stack(wkvs_p)), stack(convs, 0), jnp.stack(ssms_p),
            stack(shifts, 1), _wkv_state_from_kernel(wkv_acc), stack(convs, 1), ssm_acc)
```
